```python
import math
import jax
import jax.numpy as jnp
from jax import lax
import numpy as np

D_MODEL = 1024
BATCH = 8
SEQ = 2048
DEPTH = 2
DEC_BATCH = 32
DEC_SEQ = 4
PAST_LEN = 16384
PAGE_SIZE = 128

N_A_LAYERS = DEPTH // 2
N_B_LAYERS = DEPTH - N_A_LAYERS
RET_DK = 256
RET_DV = 512
RET_HEADS = D_MODEL // RET_DK
RET_CHUNK = 128
FOX_DH = 64
FOX_HEADS = D_MODEL // FOX_DH
Q_BLOCK = 128
D_FF = 256 * ((8 * D_MODEL // 3 + 255) // 256)
CONV_W = 3
EPS = 1e-6
ROPE_BASE = 10000.0
NEG_INF = -1e30

kernel_name = "yoco_retention_fox_convffn_step"


def rmsnorm(x, g):
    xf = x.astype(jnp.float32)
    y = xf * lax.rsqrt(jnp.mean(xf * xf, axis=-1, keepdims=True) + EPS) * g.astype(jnp.float32)
    return y.astype(x.dtype)


def rotary(x, pos):
    d2 = x.shape[-1] // 2
    inv = ROPE_BASE ** (-jnp.linspace(0.0, 1.0, d2, dtype=jnp.float32))
    ang = pos[:, None] * inv[None, :]
    cos = jnp.cos(ang)[None, :, None, :]
    sin = jnp.sin(ang)[None, :, None, :]
    x1, x2 = x[..., :d2], x[..., d2:]
    return jnp.concatenate([x1 * cos - x2 * sin, x2 * cos + x1 * sin], axis=-1)


def retention_log_decay():
    return jnp.log1p(-jnp.exp2(-5.0 - jnp.arange(RET_HEADS, dtype=jnp.float32)))


def retention_scan(q, k, v, S0, chunk):
    B, L = q.shape[0], q.shape[1]
    n = L // chunk
    lg = retention_log_decay()
    idx = jnp.arange(chunk, dtype=jnp.float32)
    diff = idx[:, None] - idx[None, :]
    dmask = jnp.where(diff >= 0, jnp.exp(lg[:, None, None] * jnp.maximum(diff, 0.0)[None]), 0.0)
    q_dec = jnp.exp(lg[None, :] * (idx[:, None] + 1.0))
    k_dec = jnp.exp(lg[None, :] * (chunk - 1.0 - idx[:, None]))
    s_dec = jnp.exp(lg * chunk)

    def to_chunks(a):
        return jnp.moveaxis(a.reshape(B, n, chunk, *a.shape[2:]), 1, 0)

    def step(S, blk):
        qc, kc, vc = blk
        att = jnp.einsum('bihd,bjhd->bhij', qc, kc) * dmask[None]
        o = (jnp.einsum('bhij,bjhe->bihe', att, vc)
             + jnp.einsum('bihd,bhde->bihe', qc * q_dec[None, :, :, None], S))
        S = S * s_dec[None, :, None, None] + jnp.einsum('bjhd,bjhe->bhde', kc * k_dec[None, :, :, None], vc)
        return S, o

    S, o = lax.scan(step, S0, (to_chunks(q), to_chunks(k), to_chunks(v)))
    o = jnp.moveaxis(o, 0, 1).reshape(B, L, RET_HEADS, RET_DV)
    return o, S


def retention_mixer(h, w_in, w_out, S0, pos0, chunk):
    B, L, _ = h.shape
    qk = RET_HEADS * RET_DK
    vd = RET_HEADS * RET_DV
    proj = h @ w_in
    q = proj[..., :qk].reshape(B, L, RET_HEADS, RET_DK).astype(jnp.float32)
    k = proj[..., qk:2 * qk].reshape(B, L, RET_HEADS, RET_DK).astype(jnp.float32)
    v = proj[..., 2 * qk:2 * qk + vd].reshape(B, L, RET_HEADS, RET_DV).astype(jnp.float32)
    g = proj[..., 2 * qk + vd:].astype(jnp.float32)
    pos = pos0 + jnp.arange(L, dtype=jnp.float32)
    q = rotary(q, pos) * (RET_DK ** -0.5)
    k = rotary(k, pos)
    o, S = retention_scan(q, k, v, S0.astype(jnp.float32), chunk)
    o = o * lax.rsqrt(jnp.mean(o * o, axis=-1, keepdims=True) + EPS)
    o = o.reshape(B, L, vd) * jax.nn.silu(g)
    return o.astype(h.dtype) @ w_out, S.astype(S0.dtype)


def conv_ffn(h, w_in, cw, cb, w_out, prev):
    L = h.shape[1]
    up = h @ w_in
    u, gv = up[..., :D_FF], up[..., D_FF:]
    ext = jnp.concatenate([prev.astype(u.dtype), u], axis=1)
    c = cb
    for j in range(CONV_W):
        c = c + cw[j] * ext[:, j:j + L]
    a = jax.nn.gelu(c) * gv
    return a @ w_out, ext[:, -(CONV_W - 1):].astype(prev.dtype)


def shared_kv(x, g, w_kv, b_f, k_gain):
    B, L, _ = x.shape
    hd = FOX_HEADS * FOX_DH
    h = rmsnorm(x, g)
    p = h @ w_kv
    k = rmsnorm(p[..., :hd].reshape(B, L, FOX_HEADS, FOX_DH), k_gain)
    v = p[..., hd:2 * hd].reshape(B, L, FOX_HEADS, FOX_DH)
    logf = jax.nn.log_sigmoid((p[..., 2 * hd:] + b_f).astype(jnp.float32))
    return k, v, logf


def fox_prompt(q, k, v, logf):
    B, L = q.shape[0], q.shape[1]
    scale = FOX_DH ** -0.5
    cT = jnp.transpose(lax.cumsum(logf, axis=1), (0, 2, 1))
    key_pos = jnp.arange(L)

    def blk(i):
        s0 = i * Q_BLOCK
        qb = lax.dynamic_slice_in_dim(q, s0, Q_BLOCK, axis=1)
        cb = lax.dynamic_slice_in_dim(cT, s0, Q_BLOCK, axis=2)
        logits = jnp.einsum('bqhd,bkhd->bhqk', qb, k).astype(jnp.float32) * scale
        logits = logits + cb[..., :, None] - cT[:, :, None, :]
        q_pos = s0 + jnp.arange(Q_BLOCK)
        mask = key_pos[None, :] <= q_pos[:, None]
        logits = jnp.where(mask[None, None], logits, NEG_INF)
        p = jax.nn.softmax(logits, axis=-1)
        return jnp.einsum('bhqk,bkhd->bqhd', p, v)

    o = lax.map(blk, jnp.arange(L // Q_BLOCK))
    return jnp.moveaxis(o, 0, 1).reshape(B, L, FOX_HEADS, FOX_DH)


def fox_sample(q, k_new, v_new, lf_new, k_past, v_past, lf_past):
    T = q.shape[1]
    scale = FOX_DH ** -0.5
    c_past = lax.cumsum(lf_past, axis=1)
    c_new = c_past[:, -1:] + lax.cumsum(lf_new, axis=1)
    cpT = jnp.transpose(c_past, (0, 2, 1))
    cnT = jnp.transpose(c_new, (0, 2, 1))
    s_p = jnp.einsum('bqhd,bkhd->bhqk', q, k_past).astype(jnp.float32) * scale
    s_p = s_p + cnT[..., :, None] - cpT[:, :, None, :]
    s_n = jnp.einsum('bqhd,bkhd->bhqk', q, k_new).astype(jnp.float32) * scale
    s_n = s_n + cnT[..., :, None] - cnT[:, :, None, :]
    tri = jnp.arange(T)[None, :] <= jnp.arange(T)[:, None]
    s_n = jnp.where(tri[None, None], s_n, NEG_INF)
    P = k_past.shape[1]
    p = jax.nn.softmax(jnp.concatenate([s_p, s_n], axis=-1), axis=-1)
    return (jnp.einsum('bhqk,bkhd->bqhd', p[..., :P], v_past)
            + jnp.einsum('bhqk,bkhd->bqhd', p[..., P:], v_new))


def setup_inputs(seed: int = 0) -> dict:
    key = jax.random.key(seed)
    ks = jax.random.split(key, 32)
    f32 = jnp.float32
    n_pages = PAST_LEN // PAGE_SIZE
    n_used = DEC_BATCH * n_pages
    n_pool = (n_used * 5) // 4
    hd = FOX_HEADS * FOX_DH
    ret_in_w = 2 * RET_HEADS * RET_DK + 2 * RET_HEADS * RET_DV

    def nrm(k, shape, scale):
        return jax.random.normal(k, shape, f32) * scale

    perm = jax.random.permutation(ks[0], n_pool)
    page_table = perm[:n_used].reshape(DEC_BATCH, n_pages).astype(jnp.int32)
    return {
        "x_prompt": nrm(ks[1], (BATCH, SEQ, D_MODEL), 1.0),
        "x_sample": nrm(ks[2], (DEC_BATCH, DEC_SEQ, D_MODEL), 1.0),
        "cache_k": nrm(ks[3], (n_pool, PAGE_SIZE, FOX_HEADS, FOX_DH), 1.0),
        "cache_v": nrm(ks[4], (n_pool, PAGE_SIZE, FOX_HEADS, FOX_DH), 1.0),
        "cache_logf": jax.nn.log_sigmoid(3.0 + jax.random.normal(ks[5], (n_pool, PAGE_SIZE, FOX_HEADS), f32)),
        "state_ret": nrm(ks[6], (N_A_LAYERS, DEC_BATCH, RET_HEADS, RET_DK, RET_DV), 0.5),
        "state_conv": nrm(ks[7], (DEPTH, DEC_BATCH, CONV_W - 1, D_FF), 1.0),
        "page_table": page_table,
        "norm_mix": 1.0 + nrm(ks[8], (DEPTH, D_MODEL), 0.02),
        "norm_ffn": 1.0 + nrm(ks[9], (DEPTH, D_MODEL), 0.02),
        "w_ret_in": nrm(ks[10], (N_A_LAYERS, D_MODEL, ret_in_w), D_MODEL ** -0.5),
        "w_ret_out": nrm(ks[11], (N_A_LAYERS, RET_HEADS * RET_DV, D_MODEL), (RET_HEADS * RET_DV) ** -0.5),
        "w_fox_q": nrm(ks[12], (N_B_LAYERS, D_MODEL, hd), D_MODEL ** -0.5),
        "q_gain": 1.0 + nrm(ks[13], (N_B_LAYERS, FOX_DH), 0.02),
        "w_fox_out": nrm(ks[14], (N_B_LAYERS, hd, D_MODEL), hd ** -0.5),
        "norm_kv": 1.0 + nrm(ks[15], (D_MODEL,), 0.02),
        "w_kv": nrm(ks[16], (D_MODEL, 2 * hd + FOX_HEADS), D_MODEL ** -0.5),
        "b_f": jnp.linspace(1.0, 5.0, FOX_HEADS, dtype=f32) + nrm(ks[17], (FOX_HEADS,), 0.1),
        "k_gain": 1.0 + nrm(ks[18], (FOX_DH,), 0.02),
        "w_ffn_in": nrm(ks[19], (DEPTH, D_MODEL, 2 * D_FF), D_MODEL ** -0.5),
        "conv_w": nrm(ks[20], (DEPTH, CONV_W, D_FF), CONV_W ** -0.5),
        "conv_b": nrm(ks[21], (DEPTH, D_FF), 0.01),
        "w_ffn_out": nrm(ks[22], (DEPTH, D_FF, D_MODEL), D_FF ** -0.5),
    }


def reference(x_prompt, x_sample, cache_k, cache_v, cache_logf, state_ret, state_conv, page_table,
              norm_mix, norm_ffn, w_ret_in, w_ret_out, w_fox_q, q_gain, w_fox_out,
              norm_kv, w_kv, b_f, k_gain, w_ffn_in, conv_w, conv_b, w_ffn_out):

    def trunk(x, ret0, conv0, pos0, chunk, attend):
        B, L, _ = x.shape
        new_ret, new_conv = [], []
        kvl = None
        for l in range(DEPTH):
            h = rmsnorm(x, norm_mix[l])
            if l < N_A_LAYERS:
                o, S = retention_mixer(h, w_ret_in[l], w_ret_out[l], ret0[l], pos0, chunk)
                new_ret.append(S)
            else:
                j = l - N_A_LAYERS
                q = rmsnorm((h @ w_fox_q[j]).reshape(B, L, FOX_HEADS, FOX_DH), q_gain[j])
                a = attend(q, *kvl)
                o = a.reshape(B, L, FOX_HEADS * FOX_DH).astype(x.dtype) @ w_fox_out[j]
            x = x + o.astype(x.dtype)
            h = rmsnorm(x, norm_ffn[l])
            f, cs = conv_ffn(h, w_ffn_in[l], conv_w[l], conv_b[l], w_ffn_out[l], conv0[l])
            new_conv.append(cs)
            x = x + f.astype(x.dtype)
            if l == N_A_LAYERS - 1:
                kvl = shared_kv(x, norm_kv, w_kv, b_f, k_gain)
        return x, jnp.stack(new_ret), jnp.stack(new_conv), kvl

    ret0_p = jnp.zeros((N_A_LAYERS, x_prompt.shape[0], RET_HEADS, RET_DK, RET_DV), state_ret.dtype)
    conv0_p = jnp.zeros((DEPTH, x_prompt.shape[0], CONV_W - 1, D_FF), state_conv.dtype)
    y_p, ret_p, conv_p, kv_p = trunk(x_prompt, ret0_p, conv0_p, 0, min(RET_CHUNK, x_prompt.shape[1]), fox_prompt)

    DB = x_sample.shape[0]
    past = page_table.shape[1] * cache_k.shape[1]
    k_past = cache_k[page_table].reshape(DB, past, FOX_HEADS, FOX_DH)
    v_past = cache_v[page_table].reshape(DB, past, FOX_HEADS, FOX_DH)
    lf_past = cache_logf[page_table].reshape(DB, past, FOX_HEADS).astype(jnp.float32)

    def attend_sample(q, k_new, v_new, lf_new):
        return fox_sample(q, k_new, v_new, lf_new, k_past, v_past, lf_past)

    y_s, ret_s, conv_s, kv_s = trunk(x_sample, state_ret, state_conv, past, x_sample.shape[1], attend_sample)
    k_p, v_p, lf_p = kv_p
    k_s, v_s, lf_s = kv_s
    return (y_p, y_s, ret_p, ret_s, conv_p, conv_s, k_p, v_p, lf_p, k_s, v_s, lf_s)
```

```python
import functools

import jax
import jax.numpy as jnp
from jax import lax
from jax.experimental import pallas as pl
from jax.experimental.pallas import tpu as pltpu

RET_DK = 256
RET_DV = 512
FOX_DH = 64
CONV_W = 3
EPS = 1e-6
ROPE_BASE = 10000.0
NEG_INF = -1e30

LANES = 128
SUBLANES = 8
VMEM_LIMIT_CAP = 56 * 1024 * 1024
VMEM_HEADROOM = 12 * 1024 * 1024

F32 = jnp.float32
BF16 = jnp.bfloat16


def _vmem_limit(block_bytes, scratch_bytes=0):
    return int(min(2 * block_bytes + scratch_bytes + VMEM_HEADROOM, VMEM_LIMIT_CAP))


def _nbytes(shape, dtype):
    n = 1
    for s in shape:
        n *= s
    return n * jnp.dtype(dtype).itemsize


def _pick_tile(n, candidates):
    for c in candidates:
        if n % c == 0:
            return c
    return n


def _dot(a, b):
    return jnp.dot(a, b, preferred_element_type=F32)


def _dot_nt(a, b):
    return lax.dot_general(a, b, (((1,), (1,)), ((), ())), preferred_element_type=F32)


def _dot_tn(a, b):
    return lax.dot_general(a, b, (((0,), (0,)), ((), ())), preferred_element_type=F32)


def _split_bf16(x):
    hi = x.astype(BF16)
    lo = (x - hi.astype(F32)).astype(BF16)
    return hi, lo


def _rms_mm_kernel(x_ref, g_ref, w_ref, *rest, n_norm_tiles, out_scale):
    if n_norm_tiles:
        grp_ref, grp_t_ref, gain_ref, o_ref, h_ref = rest
    else:
        o_ref, h_ref = rest
    j = pl.program_id(1)

    @pl.when(j == 0)
    def _():
        x = x_ref[...]
        ms = jnp.mean(x * x, axis=-1, keepdims=True)
        h_ref[...] = (x * lax.rsqrt(ms + EPS) * g_ref[...]).astype(BF16)

    acc = _dot(h_ref[...], w_ref[...])
    if not n_norm_tiles:
        o_ref[...] = acc.astype(o_ref.dtype)
        return

    @pl.when(j < n_norm_tiles)
    def _():
        sq_hi, sq_lo = _split_bf16(acc * acc)
        ssq = _dot(sq_hi, grp_ref[...]) + _dot(sq_lo, grp_ref[...])
        inv = lax.rsqrt(ssq * (1.0 / FOX_DH) + EPS)
        inv_hi, inv_lo = _split_bf16(inv)
        inv_full = _dot(inv_hi, grp_t_ref[...]) + _dot(inv_lo, grp_t_ref[...])
        y = acc * inv_full * gain_ref[...]
        if out_scale != 1.0:
            y = y * out_scale
        o_ref[...] = y.astype(o_ref.dtype)

    @pl.when(j >= n_norm_tiles)
    def _():
        o_ref[...] = acc.astype(o_ref.dtype)


def _rms_matmul(x, g, w, *, out_dtype=F32, head_norm_cols=0, head_gain=None, out_scale=1.0):
    m, k = x.shape
    n = w.shape[1]
    tm = _pick_tile(m, (1024, 512, 256, 128, 64, 32, 16, 8))
    tn = _pick_tile(n, (512, 256, 128))
    assert head_norm_cols % tn == 0
    n_norm_tiles = head_norm_cols // tn
    in_specs = [
        pl.BlockSpec((tm, k), lambda i, j: (i, 0)),
        pl.BlockSpec((1, k), lambda i, j: (0, 0)),
        pl.BlockSpec((k, tn), lambda i, j: (0, j)),
    ]
    args = [x, g.reshape(1, k).astype(F32), w]
    if n_norm_tiles:
        n_grp = tn // FOX_DH
        col = jnp.arange(tn) // FOX_DH
        grp = (col[:, None] == jnp.arange(LANES)[None, :]).astype(BF16)
        in_specs += [
            pl.BlockSpec((tn, LANES), lambda i, j: (0, 0)),
            pl.BlockSpec((LANES, tn), lambda i, j: (0, 0)),
            pl.BlockSpec((1, tn), lambda i, j: (0, 0)),
        ]
        args += [grp, grp.T, jnp.tile(head_gain.astype(F32), n_grp).reshape(1, tn)]
    blocks = (_nbytes((tm, k), x.dtype) + _nbytes((k, tn), w.dtype) + _nbytes((tm, tn), out_dtype)
              + 2 * _nbytes((tn, LANES), BF16))
    return pl.pallas_call(
        functools.partial(_rms_mm_kernel, n_norm_tiles=n_norm_tiles, out_scale=out_scale),
        grid=(m // tm, n // tn),
        in_specs=in_specs,
        out_specs=pl.BlockSpec((tm, tn), lambda i, j: (i, j)),
        out_shape=jax.ShapeDtypeStruct((m, n), out_dtype),
        scratch_shapes=[pltpu.VMEM((tm, k), BF16)],
        compiler_params=pltpu.CompilerParams(
            dimension_semantics=("parallel", "arbitrary"),
            vmem_limit_bytes=_vmem_limit(blocks, _nbytes((tm, k), BF16))),
        name="rms_matmul",
    )(*args)


def _forget_gate_kernel(x_ref, g_ref, w_ref, b_ref, o_ref):
    x = x_ref[...]
    ms = jnp.mean(x * x, axis=-1, keepdims=True)
    h = (x * lax.rsqrt(ms + EPS) * g_ref[...]).astype(BF16)
    z = _dot(h, w_ref[...]) + b_ref[...]
    o_ref[...] = jnp.minimum(z, 0.0) - jnp.log1p(jnp.exp(-jnp.abs(z)))


def _forget_gate(x, g, w_f, b_f):
    m, k = x.shape
    tm = _pick_tile(m, (1024, 512, 256, 128, 64, 32, 16, 8))
    blocks = _nbytes((tm, k), F32) + _nbytes((k, LANES), BF16) + _nbytes((tm, LANES), F32)
    return pl.pallas_call(
        _forget_gate_kernel,
        grid=(m // tm,),
        in_specs=[
            pl.BlockSpec((tm, k), lambda i: (i, 0)),
            pl.BlockSpec((1, k), lambda i: (0, 0)),
            pl.BlockSpec((k, LANES), lambda i: (0, 0)),
            pl.BlockSpec((1, LANES), lambda i: (0, 0)),
        ],
        out_specs=pl.BlockSpec((tm, LANES), lambda i: (i, 0)),
        out_shape=jax.ShapeDtypeStruct((m, LANES), F32),
        compiler_params=pltpu.CompilerParams(
            dimension_semantics=("parallel",), vmem_limit_bytes=_vmem_limit(blocks)),
        name="forget_gate",
    )(x, g.reshape(1, k).astype(F32), w_f, b_f)


def _mm_res_kernel(a_ref, w_ref, r_ref, o_ref):
    o_ref[...] = r_ref[...] + _dot(a_ref[...], w_ref[...])


def _matmul_residual(a, w, res):
    m, k = a.shape
    n = w.shape[1]
    tm = _pick_tile(m, (1024, 512, 256, 128, 64, 32, 16, 8))
    tn = _pick_tile(n, (512, 256, 128))
    blocks = _nbytes((tm, k), a.dtype) + _nbytes((k, tn), w.dtype) + 2 * _nbytes((tm, tn), F32)
    return pl.pallas_call(
        _mm_res_kernel,
        grid=(m // tm, n // tn),
        in_specs=[
            pl.BlockSpec((tm, k), lambda i, j: (i, 0)),
            pl.BlockSpec((k, tn), lambda i, j: (0, j)),
            pl.BlockSpec((tm, tn), lambda i, j: (i, j)),
        ],
        out_specs=pl.BlockSpec((tm, tn), lambda i, j: (i, j)),
        out_shape=jax.ShapeDtypeStruct((m, n), F32),
        compiler_params=pltpu.CompilerParams(
            dimension_semantics=("parallel", "parallel"), vmem_limit_bytes=_vmem_limit(blocks)),
        name="matmul_residual",
    )(a, w, res)


def _retention_kernel(lg_ref, q_ref, k_ref, v_ref, g_ref, cos_ref, sin_ref, *rest, chunk, has_state):
    if has_state:
        s0_ref, y_ref, s_ref = rest
    else:
        y_ref, s_ref = rest
    c = pl.program_id(2)
    lg = lg_ref[pl.program_id(1)]
    rows = q_ref.shape[1]
    half = RET_DK // 2

    @pl.when(c == 0)
    def _():
        if has_state:
            s_ref[...] = s0_ref[...]
        else:
            s_ref[...] = jnp.zeros_like(s_ref)

    cos = cos_ref[...]
    sin = sin_ref[...]

    def rotary(x):
        x1, x2 = x[:, :half], x[:, half:]
        return jnp.concatenate([x1 * cos - x2 * sin, x2 * cos + x1 * sin], axis=-1)

    q = rotary(q_ref[0]) * (RET_DK ** -0.5)
    k = rotary(k_ref[0])
    v = v_ref[0].astype(BF16)
    ri = lax.broadcasted_iota(jnp.int32, (rows, 1), 0).astype(F32)
    ci = lax.broadcasted_iota(jnp.int32, (1, rows), 1).astype(F32)
    q_dec = jnp.exp(lg * (ri + 1.0))
    k_dec = jnp.exp(lg * (chunk - 1.0 - ri))
    diff = ri - ci
    dmask = jnp.where(diff >= 0, jnp.exp(lg * jnp.maximum(diff, 0.0)), 0.0)
    s_dec = jnp.exp(lg * jnp.full((1, RET_DV), float(chunk), F32))

    att = _dot_nt(q.astype(BF16), k.astype(BF16)) * dmask
    state = s_ref[0, 0]
    o = _dot(att.astype(BF16), v) + _dot((q * q_dec).astype(BF16), state.astype(BF16))
    s_ref[0, 0] = state * s_dec + _dot_tn((k * k_dec).astype(BF16), v)

    o = o * lax.rsqrt(jnp.mean(o * o, axis=-1, keepdims=True) + EPS)
    gate = g_ref[0]
    y_ref[0] = (o * (gate * (1.0 / (1.0 + jnp.exp(-gate))))).astype(y_ref.dtype)


def _retention(proj, lg, cos, sin, state0, *, rows, chunk):
    b, lp, _ = proj.shape
    n_heads = lg.shape[0]
    n_chunks = lp // rows
    k_off = n_heads * RET_DK // RET_DK
    v_off = 2 * n_heads * RET_DK // RET_DV
    g_off = v_off + n_heads
    has_state = state0 is not None
    in_specs = [
        pl.BlockSpec(memory_space=pltpu.SMEM),
        pl.BlockSpec((1, rows, RET_DK), lambda i, h, c: (i, c, h)),
        pl.BlockSpec((1, rows, RET_DK), lambda i, h, c: (i, c, k_off + h)),
        pl.BlockSpec((1, rows, RET_DV), lambda i, h, c: (i, c, v_off + h)),
        pl.BlockSpec((1, rows, RET_DV), lambda i, h, c: (i, c, g_off + h)),
        pl.BlockSpec((rows, RET_DK // 2), lambda i, h, c: (c, 0)),
        pl.BlockSpec((rows, RET_DK // 2), lambda i, h, c: (c, 0)),
    ]
    args = [lg, proj, proj, proj, proj, cos, sin]
    if has_state:
        in_specs.append(pl.BlockSpec((1, 1, RET_DK, RET_DV), lambda i, h, c: (i, h, 0, 0)))
        args.append(state0)
    blocks = (2 * _nbytes((rows, RET_DK), F32) + 2 * _nbytes((rows, RET_DV), F32)
              + _nbytes((rows, RET_DK), F32) + _nbytes((rows, RET_DV), BF16)
              + 2 * _nbytes((RET_DK, RET_DV), F32))
    return pl.pallas_call(
        functools.partial(_retention_kernel, chunk=chunk, has_state=has_state),
        grid=(b, n_heads, n_chunks),
        in_specs=in_specs,
        out_specs=[
            pl.BlockSpec((1, rows, RET_DV), lambda i, h, c: (i, c, h)),
            pl.BlockSpec((1, 1, RET_DK, RET_DV), lambda i, h, c: (i, h, 0, 0)),
        ],
        out_shape=[
            jax.ShapeDtypeStruct((b, lp, n_heads * RET_DV), BF16),
            jax.ShapeDtypeStruct((b, n_heads, RET_DK, RET_DV), F32),
        ],
        compiler_params=pltpu.CompilerParams(
            dimension_semantics=("parallel", "parallel", "arbitrary"),
            vmem_limit_bytes=_vmem_limit(blocks)),
        name="retention",
    )(*args)


def _conv_gate_kernel(u_ref, halo_ref, prev_ref, gv_ref, cw_ref, cb_ref, a_ref):
    i = pl.program_id(1)
    u = u_ref[0]
    rows = u.shape[0]
    halo = jnp.where(i == 0, prev_ref[0], halo_ref[0])
    h1 = halo[SUBLANES - 1:SUBLANES]
    h2 = halo[SUBLANES - 2:SUBLANES - 1]
    row = lax.broadcasted_iota(jnp.int32, (rows, 1), 0)
    u1 = jnp.where(row >= 1, pltpu.roll(u, 1, axis=0), h1)
    u2 = jnp.where(row >= 2, pltpu.roll(u, 2, axis=0), jnp.where(row == 1, h1, h2))
    cw = cw_ref[...]
    c = cb_ref[...] + cw[0:1] * u2
    c = c + cw[1:2] * u1
    c = c + cw[2:3] * u
    a_ref[0] = (jax.nn.gelu(c) * gv_ref[0]).astype(a_ref.dtype)


def _conv_gate(up, prev8, cw8, cb, d_ff):
    b, lp, _ = up.shape
    tm = _pick_tile(lp, (512, 256, 128, 64, 32, 16, 8))
    tn = _pick_tile(d_ff, (256, 128))
    gv_off = d_ff // tn
    halo_per_tile = tm // SUBLANES
    blocks = 2 * _nbytes((tm, tn), F32) + _nbytes((tm, tn), BF16) + 4 * _nbytes((SUBLANES, tn), F32)
    return pl.pallas_call(
        _conv_gate_kernel,
        grid=(b, lp // tm, d_ff // tn),
        in_specs=[
            pl.BlockSpec((1, tm, tn), lambda n, i, j: (n, i, j)),
            pl.BlockSpec((1, SUBLANES, tn), lambda n, i, j: (n, jnp.maximum(i * halo_per_tile - 1, 0), j)),
            pl.BlockSpec((1, SUBLANES, tn), lambda n, i, j: (n, 0, j)),
            pl.BlockSpec((1, tm, tn), lambda n, i, j: (n, i, gv_off + j)),
            pl.BlockSpec((SUBLANES, tn), lambda n, i, j: (0, j)),
            pl.BlockSpec((1, tn), lambda n, i, j: (0, j)),
        ],
        out_specs=pl.BlockSpec((1, tm, tn), lambda n, i, j: (n, i, j)),
        out_shape=jax.ShapeDtypeStruct((b, lp, d_ff), BF16),
        compiler_params=pltpu.CompilerParams(
            dimension_semantics=("parallel", "parallel", "parallel"),
            vmem_limit_bytes=_vmem_limit(blocks)),
        name="conv_gate",
    )(up, up, prev8, up, cw8, cb)


def _lane_cumsum_kernel(x_ref, o_ref):
    c = x_ref[...]
    n = c.shape[-1]
    lane = lax.broadcasted_iota(jnp.int32, c.shape, 1)
    shift = 1
    while shift < n:
        c = c + jnp.where(lane >= shift, pltpu.roll(c, shift, axis=1), 0.0)
        shift *= 2
    o_ref[...] = c


def _lane_cumsum(x):
    r, n = x.shape
    tr = _pick_tile(r, (1024, 512, 256, 128, 64, 32, 16, 8))
    return pl.pallas_call(
        _lane_cumsum_kernel,
        grid=(r // tr,),
        in_specs=[pl.BlockSpec((tr, n), lambda i: (i, 0))],
        out_specs=pl.BlockSpec((tr, n), lambda i: (i, 0)),
        out_shape=jax.ShapeDtypeStruct((r, n), F32),
        compiler_params=pltpu.CompilerParams(
            dimension_semantics=("parallel",), vmem_limit_bytes=_vmem_limit(2 * _nbytes((tr, n), F32))),
        name="lane_cumsum",
    )(x)


def _fox_prompt_kernel(q_ref, k_ref, v_ref, c_ref, o_ref, m_ref, l_ref, acc_ref):
    qi = pl.program_id(2)
    ki = pl.program_id(3)
    tq = q_ref.shape[0]
    tk = k_ref.shape[0]

    @pl.when(ki == 0)
    def _():
        m_ref[...] = jnp.full_like(m_ref, NEG_INF)
        l_ref[...] = jnp.zeros_like(l_ref)
        acc_ref[...] = jnp.zeros_like(acc_ref)

    lane = lax.broadcasted_iota(jnp.int32, (1, LANES), 1)

    def step(diagonal):
        q = q_ref[...]
        k = k_ref[...].astype(BF16)
        v = v_ref[...].astype(BF16)
        for h in range(2):
            in_head = (lane < FOX_DH) if h == 0 else (lane >= FOX_DH)
            s = _dot_nt(jnp.where(in_head, q, jnp.zeros_like(q)), k)
            s = s - c_ref[0, h:h + 1, :]
            if diagonal:
                row = lax.broadcasted_iota(jnp.int32, (tq, tk), 0)
                col = lax.broadcasted_iota(jnp.int32, (tq, tk), 1)
                s = jnp.where(col <= row, s, NEG_INF)
            m_prev = m_ref[h]
            m_new = jnp.maximum(m_prev, jnp.max(s, axis=-1, keepdims=True))
            alpha = jnp.exp(m_prev - m_new)
            p = jnp.exp(s - m_new)
            l_ref[h] = alpha * l_ref[h] + jnp.sum(p, axis=-1, keepdims=True)
            acc_ref[h] = alpha * acc_ref[h] + _dot(p.astype(BF16), v)
            m_ref[h] = m_new

    @pl.when(ki < qi)
    def _():
        step(False)

    @pl.when(ki == qi)
    def _():
        step(True)
        o_ref[...] = jnp.where(lane < FOX_DH, acc_ref[0] / l_ref[0], acc_ref[1] / l_ref[1]).astype(o_ref.dtype)


def _fox_prompt(q, k, v, c_pairs, batch, seq):
    m, hd = q.shape
    n_pairs = hd // LANES
    t = _pick_tile(seq, (512, 256, 128))
    nb = seq // t
    blocks = (_nbytes((t, LANES), BF16) + 2 * _nbytes((t, LANES), F32) + _nbytes((SUBLANES, t), F32)
              + _nbytes((t, LANES), BF16))
    scratch = 4 * _nbytes((t, LANES), F32) + 2 * _nbytes((t, LANES), F32)
    return pl.pallas_call(
        _fox_prompt_kernel,
        grid=(batch, n_pairs, nb, nb),
        in_specs=[
            pl.BlockSpec((t, LANES), lambda b, p, qi, ki: (b * nb + qi, p)),
            pl.BlockSpec((t, LANES), lambda b, p, qi, ki: (b * nb + jnp.minimum(ki, qi), p)),
            pl.BlockSpec((t, LANES), lambda b, p, qi, ki: (b * nb + jnp.minimum(ki, qi), p)),
            pl.BlockSpec((1, 2, t), lambda b, p, qi, ki: (b * n_pairs + p, 0, jnp.minimum(ki, qi))),
        ],
        out_specs=pl.BlockSpec((t, LANES), lambda b, p, qi, ki: (b * nb + qi, p)),
        out_shape=jax.ShapeDtypeStruct((m, hd), BF16),
        scratch_shapes=[
            pltpu.VMEM((2, t, 1), F32),
            pltpu.VMEM((2, t, 1), F32),
            pltpu.VMEM((2, t, LANES), F32),
        ],
        compiler_params=pltpu.CompilerParams(
            dimension_semantics=("parallel", "parallel", "parallel", "arbitrary"),
            vmem_limit_bytes=_vmem_limit(blocks, scratch + 6 * _nbytes((t, t), F32))),
        name="fox_prompt",
    )(q, k, v, c_pairs)


def _fox_sample_kernel(pt_ref, q_ref, k_ref, v_ref, cw_ref, kn_ref, vn_ref, cn_ref, o_ref,
                       qbd_ref, m_ref, l_ref, acc_ref, carry_ref, kpad_ref, vpad_ref, *, n_new, n_heads):
    del pt_ref
    p = pl.program_id(1)
    n_pages = pl.num_programs(1)
    hd = n_heads * FOX_DH
    page = k_ref.shape[1]
    lane_head = lax.broadcasted_iota(jnp.int32, (n_heads, hd), 1) // FOX_DH
    row_head = lax.broadcasted_iota(jnp.int32, (n_heads, hd), 0)
    own_lanes = lane_head == row_head

    @pl.when(p == 0)
    def _():
        q = q_ref[0]
        for t in range(n_new):
            qt = jnp.broadcast_to(q[t:t + 1] * (FOX_DH ** -0.5), (n_heads, hd))
            qbd_ref[t * n_heads:(t + 1) * n_heads, :] = jnp.where(own_lanes, qt, 0.0).astype(BF16)
        m_ref[...] = jnp.full_like(m_ref, NEG_INF)
        l_ref[...] = jnp.zeros_like(l_ref)
        acc_ref[...] = jnp.zeros_like(acc_ref)
        carry_ref[...] = jnp.zeros_like(carry_ref)

    def attend(kb, vb, c_keys, mask):
        s = _dot_nt(qbd_ref[...], kb) - jnp.concatenate([c_keys] * n_new, axis=0)
        if mask is not None:
            s = jnp.where(mask, s, NEG_INF)
        m_prev = m_ref[...]
        m_new = jnp.maximum(m_prev, jnp.max(s, axis=-1, keepdims=True))
        alpha = jnp.exp(m_prev - m_new)
        pr = jnp.exp(s - m_new)
        l_ref[...] = alpha * l_ref[...] + jnp.sum(pr, axis=-1, keepdims=True)
        acc_ref[...] = alpha * acc_ref[...] + _dot(pr.astype(BF16), vb)
        m_ref[...] = m_new

    cw = cw_ref[0]
    attend(k_ref[0].astype(BF16), v_ref[0].astype(BF16), carry_ref[...] + cw, None)
    carry_ref[...] = carry_ref[...] + jnp.broadcast_to(cw[:, page - 1:page], cw.shape)

    @pl.when(p == n_pages - 1)
    def _():
        kpad_ref[...] = jnp.zeros_like(kpad_ref)
        vpad_ref[...] = jnp.zeros_like(vpad_ref)
        kpad_ref[0:SUBLANES, :] = kn_ref[0]
        vpad_ref[0:SUBLANES, :] = vn_ref[0]
        key = lax.broadcasted_iota(jnp.int32, (n_new * n_heads, page), 1)
        tok = lax.broadcasted_iota(jnp.int32, (n_new * n_heads, page), 0) // n_heads
        attend(kpad_ref[...].astype(BF16), vpad_ref[...].astype(BF16), carry_ref[...] + cn_ref[0], key <= tok)
        out = acc_ref[...] / l_ref[...]
        rows = []
        for t in range(n_new):
            blk = jnp.where(own_lanes, out[t * n_heads:(t + 1) * n_heads], 0.0)
            rows.append(jnp.sum(blk, axis=0, keepdims=True))
        rows.append(jnp.zeros((SUBLANES - n_new, hd), F32))
        o_ref[0] = jnp.concatenate(rows, axis=0).astype(o_ref.dtype)


def _fox_sample(page_table, q8, cache_k, cache_v, c_within, k_new8, v_new8, c_new, *, n_new):
    db, n_pages = page_table.shape
    _, page, hd = cache_k.shape
    n_heads = hd // FOX_DH
    assert page == LANES and n_new <= SUBLANES
    rows = n_new * n_heads
    grid_spec = pltpu.PrefetchScalarGridSpec(
        num_scalar_prefetch=1,
        grid=(db, n_pages),
        in_specs=[
            pl.BlockSpec((1, SUBLANES, hd), lambda b, p, pt: (b, 0, 0)),
            pl.BlockSpec((1, page, hd), lambda b, p, pt: (pt[b * n_pages + p], 0, 0)),
            pl.BlockSpec((1, page, hd), lambda b, p, pt: (pt[b * n_pages + p], 0, 0)),
            pl.BlockSpec((1, n_heads, page), lambda b, p, pt: (pt[b * n_pages + p], 0, 0)),
            pl.BlockSpec((1, SUBLANES, hd), lambda b, p, pt: (b, 0, 0)),
            pl.BlockSpec((1, SUBLANES, hd), lambda b, p, pt: (b, 0, 0)),
            pl.BlockSpec((1, n_heads, page), lambda b, p, pt: (b, 0, 0)),
        ],
        out_specs=pl.BlockSpec((1, SUBLANES, hd), lambda b, p, pt: (b, 0, 0)),
        scratch_shapes=[
            pltpu.VMEM((rows, hd), BF16),
            pltpu.VMEM((rows, 1), F32),
            pltpu.VMEM((rows, 1), F32),
            pltpu.VMEM((rows, hd), F32),
            pltpu.VMEM((n_heads, page), F32),
            pltpu.VMEM((page, hd), F32),
            pltpu.VMEM((page, hd), F32),
        ],
    )
    blocks = 2 * _nbytes((page, hd), F32) + 4 * _nbytes((SUBLANES, hd), F32) + 2 * _nbytes((n_heads, page), F32)
    scratch = (_nbytes((rows, hd), BF16) + _nbytes((rows, hd), F32) + 2 * _nbytes((page, hd), F32)
               + 3 * _nbytes((rows, LANES), F32))
    return pl.pallas_call(
        functools.partial(_fox_sample_kernel, n_new=n_new, n_heads=n_heads),
        grid_spec=grid_spec,
        out_shape=jax.ShapeDtypeStruct((db, SUBLANES, hd), F32),
        compiler_params=pltpu.CompilerParams(
            dimension_semantics=("parallel", "arbitrary"),
            vmem_limit_bytes=_vmem_limit(blocks, scratch)),
        name="fox_sample",
    )(page_table.reshape(-1), q8, cache_k, cache_v, c_within, k_new8, v_new8, c_new)


def _rotary_tables(pos0, n):
    half = RET_DK // 2
    inv = ROPE_BASE ** (-jnp.linspace(0.0, 1.0, half, dtype=F32))
    pos = pos0 + jnp.arange(n, dtype=F32)
    ang = pos[:, None] * inv[None, :]
    return jnp.cos(ang), jnp.sin(ang)


def _retention_log_decay(n_heads):
    return jnp.log1p(-jnp.exp2(-5.0 - jnp.arange(n_heads, dtype=F32)))


def _pad_rows(a, rows):
    pad = [(0, 0)] * a.ndim
    pad[1] = (rows - a.shape[1], 0)
    return jnp.pad(a, pad)


def _conv_ffn(x2d, batch, lp, n_valid, g, w_in, cw, cb, w_out, prev):
    d_ff = cw.shape[1]
    up = _rms_matmul(x2d, g, w_in).reshape(batch, lp, 2 * d_ff)
    prev8 = _pad_rows(prev.astype(F32), SUBLANES)
    cw8 = jnp.pad(cw.astype(F32), ((0, SUBLANES - CONV_W), (0, 0)))
    a = _conv_gate(up, prev8, cw8, cb.reshape(1, d_ff).astype(F32), d_ff)
    new_state = up[:, n_valid - (CONV_W - 1):n_valid, :d_ff]
    return _matmul_residual(a.reshape(batch * lp, d_ff), w_out, x2d), new_state


def _trunk(x, ret0, conv0, pos0, rows, chunk, weights, attend):
    (norm_mix, norm_ffn, w_ret_in, w_ret_out, w_fox_q, q_gain, w_fox_out, norm_kv, w_kv_main, w_f, b_f, k_gain,
     w_ffn_in, conv_w, conv_b, w_ffn_out) = weights
    batch, lp, d = x.shape
    n_valid = attend["n_valid"]
    n_ret_heads = d // RET_DK
    hd = w_fox_q.shape[1]
    x2d = x.reshape(batch * lp, d)

    proj = _rms_matmul(x2d, norm_mix[0], w_ret_in).reshape(batch, lp, -1)
    cos, sin = _rotary_tables(pos0, lp)
    y, ret_state = _retention(proj, _retention_log_decay(n_ret_heads), cos, sin, ret0, rows=rows, chunk=chunk)
    x2d = _matmul_residual(y.reshape(batch * lp, -1), w_ret_out, x2d)
    x2d, conv_state0 = _conv_ffn(x2d, batch, lp, n_valid, norm_ffn[0], w_ffn_in[0], conv_w[0], conv_b[0],
                                 w_ffn_out[0], conv0[0])

    kv = _rms_matmul(x2d, norm_kv, w_kv_main, head_norm_cols=hd, head_gain=k_gain)
    k2d, v2d = kv[:, :hd], kv[:, hd:]
    logf = _forget_gate(x2d, norm_kv, w_f, b_f)

    q2d = _rms_matmul(x2d, norm_mix[1], w_fox_q, out_dtype=attend["q_dtype"], head_norm_cols=hd,
                      head_gain=q_gain, out_scale=attend["q_scale"])
    a2d = attend["fn"](q2d, k2d, v2d, logf)
    x2d = _matmul_residual(a2d, w_fox_out, x2d)
    x2d, conv_state1 = _conv_ffn(x2d, batch, lp, n_valid, norm_ffn[1], w_ffn_in[1], conv_w[1], conv_b[1],
                                 w_ffn_out[1], conv0[1])
    return x2d.reshape(batch, lp, d), ret_state, jnp.stack([conv_state0, conv_state1]), (k2d, v2d, logf)


def kernel(x_prompt, x_sample, cache_k, cache_v, cache_logf, state_ret, state_conv, page_table,
           norm_mix, norm_ffn, w_ret_in, w_ret_out, w_fox_q, q_gain, w_fox_out,
           norm_kv, w_kv, b_f, k_gain, w_ffn_in, conv_w, conv_b, w_ffn_out):
    batch, seq, d = x_prompt.shape
    db, n_new, _ = x_sample.shape
    n_pool, page, n_heads, dh = cache_k.shape
    assert dh == FOX_DH and page == LANES and n_new >= CONV_W - 1
    hd = n_heads * dh
    d_ff = conv_w.shape[-1]
    n_pages = page_table.shape[1]
    past = n_pages * page

    w_f = jnp.pad(w_kv[:, 2 * hd:], ((0, 0), (0, LANES - n_heads))).astype(BF16)
    b_f_pad = jnp.pad(b_f.astype(F32), (0, LANES - n_heads)).reshape(1, LANES)
    weights = (norm_mix, norm_ffn, w_ret_in[0].astype(BF16), w_ret_out[0].astype(BF16), w_fox_q[0].astype(BF16),
               q_gain[0], w_fox_out[0].astype(BF16), norm_kv, w_kv[:, :2 * hd].astype(BF16), w_f, b_f_pad, k_gain,
               w_ffn_in.astype(BF16), conv_w, conv_b, w_ffn_out.astype(BF16))

    def attend_prompt(q2d, k2d, v2d, logf):
        lf_t = jnp.transpose(logf[:, :n_heads].reshape(batch, seq, n_heads), (0, 2, 1))
        c = _lane_cumsum(lf_t.reshape(batch * n_heads, seq))
        return _fox_prompt(q2d, k2d, v2d, c.reshape(batch * n_heads // 2, 2, seq), batch, seq)

    conv0_p = jnp.zeros((2, batch, CONV_W - 1, d_ff), F32)
    chunk_p = min(128, seq)
    y_p, ret_p, conv_p, (k_p, v_p, lf_p) = _trunk(
        x_prompt, None, conv0_p, 0.0, chunk_p, chunk_p, weights,
        dict(fn=attend_prompt, n_valid=seq, q_dtype=BF16, q_scale=FOX_DH ** -0.5))

    lp_s = SUBLANES
    x_s = jnp.pad(x_sample, ((0, 0), (0, lp_s - n_new), (0, 0)))
    lf_pool_t = jnp.transpose(cache_logf.astype(F32), (0, 2, 1)).reshape(n_pool * n_heads, page)
    c_within = _lane_cumsum(lf_pool_t).reshape(n_pool, n_heads, page)

    def attend_sample(q2d, k2d, v2d, logf):
        lf_new = logf[:, :n_heads].reshape(db, lp_s, n_heads)[:, :n_new]
        lf_new_t = jnp.pad(jnp.transpose(lf_new, (0, 2, 1)), ((0, 0), (0, 0), (0, page - n_new)))
        c_new = _lane_cumsum(lf_new_t.reshape(db * n_heads, page)).reshape(db, n_heads, page)
        a = _fox_sample(page_table, q2d.reshape(db, lp_s, hd), cache_k.reshape(n_pool, page, hd),
                        cache_v.reshape(n_pool, page, hd), c_within, k2d.reshape(db, lp_s, hd),
                        v2d.reshape(db, lp_s, hd), c_new, n_new=n_new)
        return a.reshape(db * lp_s, hd).astype(BF16)

    y_s, ret_s, conv_s, (k_s, v_s, lf_s) = _trunk(
        x_s, state_ret[0], state_conv, float(past), lp_s, n_new, weights,
        dict(fn=attend_sample, n_valid=n_new, q_dtype=F32, q_scale=1.0))

    def heads(a, b, l):
        return a.reshape(b, l, n_heads, dh)

    return (y_p, y_s[:, :n_new], ret_p[None], ret_s[None], conv_p, conv_s,
            heads(k_p, batch, seq), heads(v_p, batch, seq), lf_p[:, :n_heads].reshape(batch, seq, n_heads),
            heads(k_s, db, lp_s)[:, :n_new], heads(v_s, db, lp_s)[:, :n_new],
            lf_s[:, :n_heads].reshape(db, lp_s, n_heads)[:, :n_new])
```

```python
import functools

import jax
import jax.numpy as jnp
from jax import lax
from jax.experimental import pallas as pl
from jax.experimental.pallas import tpu as pltpu

RET_DK = 256
RET_DV = 512
FOX_DH = 64
CONV_W = 3
EPS = 1e-6
ROPE_BASE = 10000.0
NEG_INF = -1e30

LANES = 128
SUBLANES = 8
VMEM_LIMIT_CAP = 56 * 1024 * 1024
VMEM_HEADROOM = 12 * 1024 * 1024
PAGES_PER_STEP = 8

F32 = jnp.float32
BF16 = jnp.bfloat16


def _vmem_limit(block_bytes, scratch_bytes=0):
    return int(min(2 * block_bytes + scratch_bytes + VMEM_HEADROOM, VMEM_LIMIT_CAP))


def _nbytes(shape, dtype):
    n = 1
    for s in shape:
        n *= s
    return n * jnp.dtype(dtype).itemsize


def _pick_tile(n, candidates):
    for c in candidates:
        if n % c == 0:
            return c
    return n


def _dot(a, b):
    return jnp.dot(a, b, preferred_element_type=F32)


def _dot_nt(a, b):
    return lax.dot_general(a, b, (((1,), (1,)), ((), ())), preferred_element_type=F32)


def _dot_tn(a, b):
    return lax.dot_general(a, b, (((0,), (0,)), ((), ())), preferred_element_type=F32)


def _split_bf16(x):
    hi = x.astype(BF16)
    lo = (x - hi.astype(F32)).astype(BF16)
    return hi, lo


def _rms_mm_kernel(x_ref, g_ref, w_ref, *rest, n_norm_tiles, out_scale, has_extra):
    if has_extra:
        grp_ref, grp_t_ref, gain_ref, parts_ref, place_ref, const_ref, o_ref, h_ref = rest
    elif n_norm_tiles:
        grp_ref, grp_t_ref, gain_ref, o_ref, h_ref = rest
    else:
        o_ref, h_ref = rest
    j = pl.program_id(1)

    @pl.when(j == 0)
    def _():
        x = x_ref[...]
        ms = jnp.mean(x * x, axis=-1, keepdims=True)
        h_ref[...] = (x * lax.rsqrt(ms + EPS) * g_ref[...]).astype(BF16)

    acc = _dot(h_ref[...], w_ref[...])
    if not n_norm_tiles:
        o_ref[...] = acc.astype(o_ref.dtype)
        return

    @pl.when(j < n_norm_tiles)
    def _():
        sq_hi, sq_lo = _split_bf16(acc * acc)
        ssq = _dot(sq_hi, grp_ref[...]) + _dot(sq_lo, grp_ref[...])
        inv = lax.rsqrt(ssq * (1.0 / FOX_DH) + EPS)
        inv_hi, inv_lo = _split_bf16(inv)
        inv_full = _dot(inv_hi, grp_t_ref[...]) + _dot(inv_lo, grp_t_ref[...])
        y = acc * inv_full * gain_ref[...]
        if out_scale != 1.0:
            y = y * out_scale
        if has_extra:
            y = y + _dot(parts_ref[...], place_ref[...]) + const_ref[...]
        o_ref[...] = y.astype(o_ref.dtype)

    @pl.when(j >= n_norm_tiles)
    def _():
        o_ref[...] = acc.astype(o_ref.dtype)


def _rms_matmul(x, g, w, *, out_dtype=F32, head_norm_cols=0, head_slot=FOX_DH, gain_row=None, out_scale=1.0,
                extra=None):
    m, k = x.shape
    n = w.shape[1]
    tm = _pick_tile(m, (1024, 512, 256, 128, 64, 32, 16, 8))
    tn = _pick_tile(n, (512, 256, 128))
    assert head_norm_cols % tn == 0
    n_norm_tiles = head_norm_cols // tn
    in_specs = [
        pl.BlockSpec((tm, k), lambda i, j: (i, 0)),
        pl.BlockSpec((1, k), lambda i, j: (0, 0)),
        pl.BlockSpec((k, tn), lambda i, j: (0, j)),
    ]
    args = [x, g.reshape(1, k).astype(F32), w]
    if n_norm_tiles:
        col = jnp.arange(tn) // head_slot
        grp = (col[:, None] == jnp.arange(LANES)[None, :]).astype(BF16)
        in_specs += [
            pl.BlockSpec((tn, LANES), lambda i, j: (0, 0)),
            pl.BlockSpec((LANES, tn), lambda i, j: (0, 0)),
            pl.BlockSpec((1, tn), lambda i, j: (0, j)),
        ]
        args += [grp, grp.T, gain_row]
    if extra is not None:
        assert n_norm_tiles == n // tn
        in_specs += [
            pl.BlockSpec((tm, LANES), lambda i, j: (i, 0)),
            pl.BlockSpec((LANES, tn), lambda i, j: (0, j)),
            pl.BlockSpec((1, tn), lambda i, j: (0, j)),
        ]
        args += list(extra)
    blocks = (_nbytes((tm, k), x.dtype) + _nbytes((k, tn), w.dtype) + _nbytes((tm, tn), out_dtype)
              + 3 * _nbytes((tn, LANES), BF16) + _nbytes((tm, LANES), BF16))
    return pl.pallas_call(
        functools.partial(_rms_mm_kernel, n_norm_tiles=n_norm_tiles, out_scale=out_scale,
                          has_extra=extra is not None),
        grid=(m // tm, n // tn),
        in_specs=in_specs,
        out_specs=pl.BlockSpec((tm, tn), lambda i, j: (i, j)),
        out_shape=jax.ShapeDtypeStruct((m, n), out_dtype),
        scratch_shapes=[pltpu.VMEM((tm, k), BF16)],
        compiler_params=pltpu.CompilerParams(
            dimension_semantics=("parallel", "arbitrary"),
            vmem_limit_bytes=_vmem_limit(blocks, _nbytes((tm, k), BF16))),
        name="rms_matmul",
    )(*args)


def _forget_gate_kernel(x_ref, g_ref, w_ref, b_ref, o_ref):
    x = x_ref[...]
    ms = jnp.mean(x * x, axis=-1, keepdims=True)
    h = (x * lax.rsqrt(ms + EPS) * g_ref[...]).astype(BF16)
    z = _dot(h, w_ref[...]) + b_ref[...]
    o_ref[...] = jnp.minimum(z, 0.0) - jnp.log1p(jnp.exp(-jnp.abs(z)))


def _forget_gate(x, g, w_f, b_f):
    m, k = x.shape
    tm = _pick_tile(m, (1024, 512, 256, 128, 64, 32, 16, 8))
    blocks = _nbytes((tm, k), F32) + _nbytes((k, LANES), BF16) + _nbytes((tm, LANES), F32)
    return pl.pallas_call(
        _forget_gate_kernel,
        grid=(m // tm,),
        in_specs=[
            pl.BlockSpec((tm, k), lambda i: (i, 0)),
            pl.BlockSpec((1, k), lambda i: (0, 0)),
            pl.BlockSpec((k, LANES), lambda i: (0, 0)),
            pl.BlockSpec((1, LANES), lambda i: (0, 0)),
        ],
        out_specs=pl.BlockSpec((tm, LANES), lambda i: (i, 0)),
        out_shape=jax.ShapeDtypeStruct((m, LANES), F32),
        compiler_params=pltpu.CompilerParams(
            dimension_semantics=("parallel",), vmem_limit_bytes=_vmem_limit(blocks)),
        name="forget_gate",
    )(x, g.reshape(1, k).astype(F32), w_f, b_f)


def _mm_res_kernel(a_ref, w_ref, r_ref, o_ref):
    o_ref[...] = r_ref[...] + _dot(a_ref[...], w_ref[...])


def _matmul_residual(a, w, res):
    m, k = a.shape
    n = w.shape[1]
    tm = _pick_tile(m, (1024, 512, 256, 128, 64, 32, 16, 8))
    tn = _pick_tile(n, (512, 256, 128))
    blocks = _nbytes((tm, k), a.dtype) + _nbytes((k, tn), w.dtype) + 2 * _nbytes((tm, tn), F32)
    return pl.pallas_call(
        _mm_res_kernel,
        grid=(m // tm, n // tn),
        in_specs=[
            pl.BlockSpec((tm, k), lambda i, j: (i, 0)),
            pl.BlockSpec((k, tn), lambda i, j: (0, j)),
            pl.BlockSpec((tm, tn), lambda i, j: (i, j)),
        ],
        out_specs=pl.BlockSpec((tm, tn), lambda i, j: (i, j)),
        out_shape=jax.ShapeDtypeStruct((m, n), F32),
        compiler_params=pltpu.CompilerParams(
            dimension_semantics=("parallel", "parallel"), vmem_limit_bytes=_vmem_limit(blocks)),
        name="matmul_residual",
    )(a, w, res)


def _retention_kernel(lg_ref, q_ref, k_ref, v_ref, g_ref, cos_ref, sin_ref, *rest, chunk, has_state):
    if has_state:
        s0_ref, y_ref, s_ref = rest
    else:
        y_ref, s_ref = rest
    c = pl.program_id(2)
    lg = lg_ref[pl.program_id(1)]
    rows = q_ref.shape[1]
    half = RET_DK // 2

    @pl.when(c == 0)
    def _():
        if has_state:
            s_ref[...] = s0_ref[...]
        else:
            s_ref[...] = jnp.zeros_like(s_ref)

    cos = cos_ref[...]
    sin = sin_ref[...]

    def rotary(x):
        x1, x2 = x[:, :half], x[:, half:]
        return jnp.concatenate([x1 * cos - x2 * sin, x2 * cos + x1 * sin], axis=-1)

    q = rotary(q_ref[0]) * (RET_DK ** -0.5)
    k = rotary(k_ref[0])
    v = v_ref[0].astype(BF16)
    ri = lax.broadcasted_iota(jnp.int32, (rows, 1), 0).astype(F32)
    ci = lax.broadcasted_iota(jnp.int32, (1, rows), 1).astype(F32)
    q_dec = jnp.exp(lg * (ri + 1.0))
    k_dec = jnp.exp(lg * (chunk - 1.0 - ri))
    diff = ri - ci
    dmask = jnp.where(diff >= 0, jnp.exp(lg * jnp.maximum(diff, 0.0)), 0.0)
    s_dec = jnp.exp(lg * jnp.full((1, RET_DV), float(chunk), F32))

    att = _dot_nt(q.astype(BF16), k.astype(BF16)) * dmask
    state = s_ref[0, 0]
    o = _dot(att.astype(BF16), v) + _dot((q * q_dec).astype(BF16), state.astype(BF16))
    s_ref[0, 0] = state * s_dec + _dot_tn((k * k_dec).astype(BF16), v)

    o = o * lax.rsqrt(jnp.mean(o * o, axis=-1, keepdims=True) + EPS)
    gate = g_ref[0]
    y_ref[0] = (o * (gate * (1.0 / (1.0 + jnp.exp(-gate))))).astype(y_ref.dtype)


def _retention(proj, lg, cos, sin, state0, *, rows, chunk):
    b, lp, _ = proj.shape
    n_heads = lg.shape[0]
    n_chunks = lp // rows
    k_off = n_heads * RET_DK // RET_DK
    v_off = 2 * n_heads * RET_DK // RET_DV
    g_off = v_off + n_heads
    has_state = state0 is not None
    in_specs = [
        pl.BlockSpec(memory_space=pltpu.SMEM),
        pl.BlockSpec((1, rows, RET_DK), lambda i, h, c: (i, c, h)),
        pl.BlockSpec((1, rows, RET_DK), lambda i, h, c: (i, c, k_off + h)),
        pl.BlockSpec((1, rows, RET_DV), lambda i, h, c: (i, c, v_off + h)),
        pl.BlockSpec((1, rows, RET_DV), lambda i, h, c: (i, c, g_off + h)),
        pl.BlockSpec((rows, RET_DK // 2), lambda i, h, c: (c, 0)),
        pl.BlockSpec((rows, RET_DK // 2), lambda i, h, c: (c, 0)),
    ]
    args = [lg, proj, proj, proj, proj, cos, sin]
    if has_state:
        in_specs.append(pl.BlockSpec((1, 1, RET_DK, RET_DV), lambda i, h, c: (i, h, 0, 0)))
        args.append(state0)
    blocks = (2 * _nbytes((rows, RET_DK), F32) + 2 * _nbytes((rows, RET_DV), F32)
              + _nbytes((rows, RET_DK), F32) + _nbytes((rows, RET_DV), BF16)
              + 2 * _nbytes((RET_DK, RET_DV), F32))
    return pl.pallas_call(
        functools.partial(_retention_kernel, chunk=chunk, has_state=has_state),
        grid=(b, n_heads, n_chunks),
        in_specs=in_specs,
        out_specs=[
            pl.BlockSpec((1, rows, RET_DV), lambda i, h, c: (i, c, h)),
            pl.BlockSpec((1, 1, RET_DK, RET_DV), lambda i, h, c: (i, h, 0, 0)),
        ],
        out_shape=[
            jax.ShapeDtypeStruct((b, lp, n_heads * RET_DV), BF16),
            jax.ShapeDtypeStruct((b, n_heads, RET_DK, RET_DV), F32),
        ],
        compiler_params=pltpu.CompilerParams(
            dimension_semantics=("parallel", "parallel", "arbitrary"),
            vmem_limit_bytes=_vmem_limit(blocks)),
        name="retention",
    )(*args)


def _conv_gate_kernel(u_ref, halo_ref, prev_ref, gv_ref, cw_ref, cb_ref, a_ref):
    i = pl.program_id(1)
    u = u_ref[0]
    rows = u.shape[0]
    halo = jnp.where(i == 0, prev_ref[0], halo_ref[0])
    h1 = halo[SUBLANES - 1:SUBLANES]
    h2 = halo[SUBLANES - 2:SUBLANES - 1]
    row = lax.broadcasted_iota(jnp.int32, (rows, 1), 0)
    u1 = jnp.where(row >= 1, pltpu.roll(u, 1, axis=0), h1)
    u2 = jnp.where(row >= 2, pltpu.roll(u, 2, axis=0), jnp.where(row == 1, h1, h2))
    cw = cw_ref[...]
    c = cb_ref[...] + cw[0:1] * u2
    c = c + cw[1:2] * u1
    c = c + cw[2:3] * u
    a_ref[0] = (jax.nn.gelu(c) * gv_ref[0]).astype(a_ref.dtype)


def _conv_gate(up, prev8, cw8, cb, d_ff):
    b, lp, _ = up.shape
    tm = _pick_tile(lp, (512, 256, 128, 64, 32, 16, 8))
    tn = _pick_tile(d_ff, (256, 128))
    gv_off = d_ff // tn
    halo_per_tile = tm // SUBLANES
    blocks = 2 * _nbytes((tm, tn), F32) + _nbytes((tm, tn), BF16) + 4 * _nbytes((SUBLANES, tn), F32)
    return pl.pallas_call(
        _conv_gate_kernel,
        grid=(b, lp // tm, d_ff // tn),
        in_specs=[
            pl.BlockSpec((1, tm, tn), lambda n, i, j: (n, i, j)),
            pl.BlockSpec((1, SUBLANES, tn), lambda n, i, j: (n, jnp.maximum(i * halo_per_tile - 1, 0), j)),
            pl.BlockSpec((1, SUBLANES, tn), lambda n, i, j: (n, 0, j)),
            pl.BlockSpec((1, tm, tn), lambda n, i, j: (n, i, gv_off + j)),
            pl.BlockSpec((SUBLANES, tn), lambda n, i, j: (0, j)),
            pl.BlockSpec((1, tn), lambda n, i, j: (0, j)),
        ],
        out_specs=pl.BlockSpec((1, tm, tn), lambda n, i, j: (n, i, j)),
        out_shape=jax.ShapeDtypeStruct((b, lp, d_ff), BF16),
        compiler_params=pltpu.CompilerParams(
            dimension_semantics=("parallel", "parallel", "parallel"),
            vmem_limit_bytes=_vmem_limit(blocks)),
        name="conv_gate",
    )(up, up, prev8, up, cw8, cb)


def _lane_cumsum_kernel(x_ref, *o_refs, split_negated):
    c = x_ref[...]
    n = c.shape[-1]
    lane = lax.broadcasted_iota(jnp.int32, c.shape, 1)
    shift = 1
    while shift < n:
        c = c + jnp.where(lane >= shift, pltpu.roll(c, shift, axis=1), 0.0)
        shift *= 2
    if not split_negated:
        o_refs[0][...] = c
        return
    hi_ref, mid_ref, lo_ref = o_refs
    hi = (-c).astype(BF16)
    rem = -c - hi.astype(F32)
    mid, lo = _split_bf16(rem)
    hi_ref[...] = hi
    mid_ref[...] = mid
    lo_ref[...] = lo


def _lane_cumsum(x, split_negated=False):
    r, n = x.shape
    tr = _pick_tile(r, (1024, 512, 256, 128, 64, 32, 16, 8))
    spec = pl.BlockSpec((tr, n), lambda i: (i, 0))
    if split_negated:
        out_specs, out_shape = [spec] * 3, [jax.ShapeDtypeStruct((r, n), BF16)] * 3
    else:
        out_specs, out_shape = spec, jax.ShapeDtypeStruct((r, n), F32)
    return pl.pallas_call(
        functools.partial(_lane_cumsum_kernel, split_negated=split_negated),
        grid=(r // tr,),
        in_specs=[spec],
        out_specs=out_specs,
        out_shape=out_shape,
        compiler_params=pltpu.CompilerParams(
            dimension_semantics=("parallel",), vmem_limit_bytes=_vmem_limit(2 * _nbytes((tr, n), F32))),
        name="lane_cumsum",
    )(x)


def _fox_prompt_kernel(qi_ref, ki_ref, qa_ref, ka_ref, vt_ref, o_ref, m_ref, l_ref, acc_ref):
    step_id = pl.program_id(2)
    qi = qi_ref[step_id]
    ki = ki_ref[step_id]
    tq = qa_ref.shape[0]
    tk = ka_ref.shape[0]

    @pl.when(ki == 0)
    def _():
        m_ref[...] = jnp.full_like(m_ref, NEG_INF)
        l_ref[...] = jnp.zeros_like(l_ref)
        acc_ref[...] = jnp.zeros_like(acc_ref)

    def step(diagonal):
        for h in range(2):
            s = _dot_nt(ka_ref[:, h * LANES:(h + 1) * LANES], qa_ref[:, h * LANES:(h + 1) * LANES])
            if diagonal:
                key = lax.broadcasted_iota(jnp.int32, (tk, tq), 0)
                qry = lax.broadcasted_iota(jnp.int32, (tk, tq), 1)
                s = jnp.where(key <= qry, s, NEG_INF)
            m_prev = m_ref[h]
            m_new = jnp.maximum(m_prev, jnp.max(s, axis=0, keepdims=True))
            alpha = jnp.exp(m_prev - m_new)
            p = jnp.exp(s - m_new)
            l_ref[h] = alpha * l_ref[h] + jnp.sum(p, axis=0, keepdims=True)
            vt = vt_ref[0, h * FOX_DH:(h + 1) * FOX_DH, :].astype(BF16)
            acc_ref[h] = alpha * acc_ref[h] + _dot(vt, p.astype(BF16))
            m_ref[h] = m_new

    @pl.when(ki < qi)
    def _():
        step(False)

    @pl.when(ki == qi)
    def _():
        step(True)
        out_t = jnp.concatenate([acc_ref[0] / l_ref[0], acc_ref[1] / l_ref[1]], axis=0)
        o_ref[...] = out_t.T.astype(o_ref.dtype)


def _fox_prompt(qa, ka, vt, batch, seq):
    m = qa.shape[0]
    hd = vt.shape[1]
    n_pairs = hd // LANES
    t = _pick_tile(seq, (512, 256, 128))
    nb = seq // t
    tri = [(qi, ki) for qi in range(nb) for ki in range(qi + 1)]
    qi_tab = jnp.asarray([s[0] for s in tri], jnp.int32)
    ki_tab = jnp.asarray([s[1] for s in tri], jnp.int32)
    grid_spec = pltpu.PrefetchScalarGridSpec(
        num_scalar_prefetch=2,
        grid=(batch, n_pairs, len(tri)),
        in_specs=[
            pl.BlockSpec((t, 2 * LANES), lambda b, p, s, qt, kt: (b * nb + qt[s], p)),
            pl.BlockSpec((t, 2 * LANES), lambda b, p, s, qt, kt: (b * nb + kt[s], p)),
            pl.BlockSpec((1, LANES, t), lambda b, p, s, qt, kt: (b, p, kt[s])),
        ],
        out_specs=pl.BlockSpec((t, LANES), lambda b, p, s, qt, kt: (b * nb + qt[s], p)),
        scratch_shapes=[
            pltpu.VMEM((2, 1, t), F32),
            pltpu.VMEM((2, 1, t), F32),
            pltpu.VMEM((2, FOX_DH, t), F32),
        ],
    )
    blocks = 2 * _nbytes((t, 2 * LANES), BF16) + _nbytes((LANES, t), F32) + _nbytes((t, LANES), BF16)
    scratch = 2 * _nbytes((FOX_DH, t), F32) + 4 * _nbytes((SUBLANES, t), F32)
    return pl.pallas_call(
        _fox_prompt_kernel,
        grid_spec=grid_spec,
        out_shape=jax.ShapeDtypeStruct((m, hd), BF16),
        compiler_params=pltpu.CompilerParams(
            dimension_semantics=("parallel", "parallel", "arbitrary"),
            vmem_limit_bytes=_vmem_limit(blocks, scratch + 6 * _nbytes((t, t), F32))),
        name="fox_prompt",
    )(qi_tab, ki_tab, qa, ka, vt)


def _fox_sample_kernel(pt_ref, q_ref, *refs, n_new, n_heads, group):
    del pt_ref
    kt_refs, vt_refs, cw_refs = refs[:group], refs[group:2 * group], refs[2 * group:3 * group]
    (kn_ref, vn_ref, cn_ref, o_ref, qbd_ref, m_ref, l_ref, acc_ref, carry_ref, kpad_ref, vpad_ref) = refs[3 * group:]
    p = pl.program_id(1)
    n_steps = pl.num_programs(1)
    hd = n_heads * FOX_DH
    page = kt_refs[0].shape[2]
    lane_head = lax.broadcasted_iota(jnp.int32, (n_heads, hd), 1) // FOX_DH
    row_head = lax.broadcasted_iota(jnp.int32, (n_heads, hd), 0)
    own_lanes = lane_head == row_head

    @pl.when(p == 0)
    def _():
        q = q_ref[0]
        for t in range(n_new):
            qt = jnp.broadcast_to(q[t:t + 1] * (FOX_DH ** -0.5), (n_heads, hd))
            qbd_ref[t * n_heads:(t + 1) * n_heads, :] = jnp.where(own_lanes, qt, 0.0).astype(BF16)
        m_ref[...] = jnp.full_like(m_ref, NEG_INF)
        l_ref[...] = jnp.zeros_like(l_ref)
        acc_ref[...] = jnp.zeros_like(acc_ref)
        carry_ref[...] = jnp.zeros_like(carry_ref)

    def softmax_update(s, weighted_values):
        m_prev = m_ref[...]
        m_new = jnp.maximum(m_prev, jnp.max(s, axis=-1, keepdims=True))
        alpha = jnp.exp(m_prev - m_new)
        pr = jnp.exp(s - m_new)
        l_ref[...] = alpha * l_ref[...] + jnp.sum(pr, axis=-1, keepdims=True)
        acc_ref[...] = alpha * acc_ref[...] + weighted_values(pr.astype(BF16))
        m_ref[...] = m_new

    qbd = qbd_ref[...]
    carry = carry_ref[...]
    scores = []
    for g in range(group):
        cw = cw_refs[g][0]
        s = _dot(qbd, kt_refs[g][0].astype(BF16))
        scores.append(s - jnp.concatenate([carry + cw] * n_new, axis=0))
        carry = carry + jnp.broadcast_to(cw[:, page - 1:page], cw.shape)
    carry_ref[...] = carry

    def cached_values(pr):
        out = _dot_nt(pr[:, 0:page], vt_refs[0][0].astype(BF16))
        for g in range(1, group):
            out = out + _dot_nt(pr[:, g * page:(g + 1) * page], vt_refs[g][0].astype(BF16))
        return out

    softmax_update(jnp.concatenate(scores, axis=1), cached_values)

    @pl.when(p == n_steps - 1)
    def _():
        kpad_ref[...] = jnp.zeros_like(kpad_ref)
        vpad_ref[...] = jnp.zeros_like(vpad_ref)
        kpad_ref[0:SUBLANES, :] = kn_ref[0]
        vpad_ref[0:SUBLANES, :] = vn_ref[0]
        key = lax.broadcasted_iota(jnp.int32, (n_new * n_heads, page), 1)
        tok = lax.broadcasted_iota(jnp.int32, (n_new * n_heads, page), 0) // n_heads
        c_keys = jnp.concatenate([carry_ref[...] + cn_ref[0]] * n_new, axis=0)
        s_new = _dot_nt(qbd_ref[...], kpad_ref[...].astype(BF16)) - c_keys
        softmax_update(jnp.where(key <= tok, s_new, NEG_INF), lambda pr: _dot(pr, vpad_ref[...].astype(BF16)))
        out = acc_ref[...] / l_ref[...]
        rows = []
        for t in range(n_new):
            blk = jnp.where(own_lanes, out[t * n_heads:(t + 1) * n_heads], 0.0)
            rows.append(jnp.sum(blk, axis=0, keepdims=True))
        rows.append(jnp.zeros((SUBLANES - n_new, hd), F32))
        o_ref[0] = jnp.concatenate(rows, axis=0).astype(o_ref.dtype)


def _page_index_map(n_pages, group, g):
    return lambda b, p, pt: (pt[b * n_pages + p * group + g], 0, 0)


def _fox_sample(page_table, q8, cache_kt, cache_vt, c_within, k_new8, v_new8, c_new, *, n_new):
    db, n_pages = page_table.shape
    _, hd, page = cache_kt.shape
    n_heads = hd // FOX_DH
    assert page == LANES and n_new <= SUBLANES
    rows = n_new * n_heads
    group = _pick_tile(n_pages, (PAGES_PER_STEP, 4, 2, 1))
    page_specs = [pl.BlockSpec((1, hd, page), _page_index_map(n_pages, group, g)) for g in range(group)]
    cw_specs = [pl.BlockSpec((1, n_heads, page), _page_index_map(n_pages, group, g)) for g in range(group)]
    grid_spec = pltpu.PrefetchScalarGridSpec(
        num_scalar_prefetch=1,
        grid=(db, n_pages // group),
        in_specs=[pl.BlockSpec((1, SUBLANES, hd), lambda b, p, pt: (b, 0, 0))] + page_specs + page_specs + cw_specs + [
            pl.BlockSpec((1, SUBLANES, hd), lambda b, p, pt: (b, 0, 0)),
            pl.BlockSpec((1, SUBLANES, hd), lambda b, p, pt: (b, 0, 0)),
            pl.BlockSpec((1, n_heads, page), lambda b, p, pt: (b, 0, 0)),
        ],
        out_specs=pl.BlockSpec((1, SUBLANES, hd), lambda b, p, pt: (b, 0, 0)),
        scratch_shapes=[
            pltpu.VMEM((rows, hd), BF16),
            pltpu.VMEM((rows, 1), F32),
            pltpu.VMEM((rows, 1), F32),
            pltpu.VMEM((rows, hd), F32),
            pltpu.VMEM((n_heads, page), F32),
            pltpu.VMEM((page, hd), F32),
            pltpu.VMEM((page, hd), F32),
        ],
    )
    blocks = (2 * group * _nbytes((page, hd), F32) + 4 * _nbytes((SUBLANES, hd), F32)
              + (group + 1) * _nbytes((n_heads, page), F32))
    scratch = (_nbytes((rows, hd), BF16) + _nbytes((rows, hd), F32) + 2 * _nbytes((page, hd), F32)
               + 3 * _nbytes((rows, LANES), F32))
    return pl.pallas_call(
        functools.partial(_fox_sample_kernel, n_new=n_new, n_heads=n_heads, group=group),
        grid_spec=grid_spec,
        out_shape=jax.ShapeDtypeStruct((db, SUBLANES, hd), F32),
        compiler_params=pltpu.CompilerParams(
            dimension_semantics=("parallel", "arbitrary"),
            vmem_limit_bytes=_vmem_limit(blocks, scratch)),
        name="fox_sample",
    )(page_table.reshape(-1), q8, *([cache_kt] * group), *([cache_vt] * group), *([c_within] * group),
      k_new8, v_new8, c_new)


def _rotary_tables(pos0, n):
    half = RET_DK // 2
    inv = ROPE_BASE ** (-jnp.linspace(0.0, 1.0, half, dtype=F32))
    pos = pos0 + jnp.arange(n, dtype=F32)
    ang = pos[:, None] * inv[None, :]
    return jnp.cos(ang), jnp.sin(ang)


def _retention_log_decay(n_heads):
    return jnp.log1p(-jnp.exp2(-5.0 - jnp.arange(n_heads, dtype=F32)))


def _pad_rows(a, rows):
    pad = [(0, 0)] * a.ndim
    pad[1] = (rows - a.shape[1], 0)
    return jnp.pad(a, pad)


def _conv_ffn(x2d, batch, lp, n_valid, g, w_in, cw, cb, w_out, prev):
    d_ff = cw.shape[1]
    up = _rms_matmul(x2d, g, w_in).reshape(batch, lp, 2 * d_ff)
    prev8 = _pad_rows(prev.astype(F32), SUBLANES)
    cw8 = jnp.pad(cw.astype(F32), ((0, SUBLANES - CONV_W), (0, 0)))
    a = _conv_gate(up, prev8, cw8, cb.reshape(1, d_ff).astype(F32), d_ff)
    new_state = up[:, n_valid - (CONV_W - 1):n_valid, :d_ff]
    return _matmul_residual(a.reshape(batch * lp, d_ff), w_out, x2d), new_state


def _trunk(x, ret0, conv0, pos0, rows, chunk, n_valid, weights, shared_kv_and_attend):
    (norm_mix, norm_ffn, w_ret_in, w_ret_out, w_fox_out, w_ffn_in, conv_w, conv_b, w_ffn_out) = weights
    batch, lp, d = x.shape
    n_ret_heads = d // RET_DK
    x2d = x.reshape(batch * lp, d)

    proj = _rms_matmul(x2d, norm_mix[0], w_ret_in).reshape(batch, lp, -1)
    cos, sin = _rotary_tables(pos0, lp)
    y, ret_state = _retention(proj, _retention_log_decay(n_ret_heads), cos, sin, ret0, rows=rows, chunk=chunk)
    x2d = _matmul_residual(y.reshape(batch * lp, -1), w_ret_out, x2d)
    x2d, conv_state0 = _conv_ffn(x2d, batch, lp, n_valid, norm_ffn[0], w_ffn_in[0], conv_w[0], conv_b[0],
                                 w_ffn_out[0], conv0[0])

    a2d, (k2d, v2d, logf) = shared_kv_and_attend(x2d)
    x2d = _matmul_residual(a2d, w_fox_out, x2d)
    x2d, conv_state1 = _conv_ffn(x2d, batch, lp, n_valid, norm_ffn[1], w_ffn_in[1], conv_w[1], conv_b[1],
                                 w_ffn_out[1], conv0[1])
    return x2d.reshape(batch, lp, d), ret_state, jnp.stack([conv_state0, conv_state1]), (k2d, v2d, logf)


def kernel(x_prompt, x_sample, cache_k, cache_v, cache_logf, state_ret, state_conv, page_table,
           norm_mix, norm_ffn, w_ret_in, w_ret_out, w_fox_q, q_gain, w_fox_out,
           norm_kv, w_kv, b_f, k_gain, w_ffn_in, conv_w, conv_b, w_ffn_out):
    batch, seq, d = x_prompt.shape
    db, n_new, _ = x_sample.shape
    n_pool, page, n_heads, dh = cache_k.shape
    assert dh == FOX_DH and page == LANES and n_new >= CONV_W - 1
    hd = n_heads * dh
    d_ff = conv_w.shape[-1]
    n_pages = page_table.shape[1]
    past = n_pages * page

    w_f = jnp.pad(w_kv[:, 2 * hd:], ((0, 0), (0, LANES - n_heads))).astype(BF16)
    b_f_pad = jnp.pad(b_f.astype(F32), (0, LANES - n_heads)).reshape(1, LANES)
    w_kv_main = w_kv[:, :2 * hd].astype(BF16)
    w_q = w_fox_q[0].astype(BF16)
    weights = (norm_mix, norm_ffn, w_ret_in[0].astype(BF16), w_ret_out[0].astype(BF16), w_fox_out[0].astype(BF16),
               w_ffn_in.astype(BF16), conv_w, conv_b, w_ffn_out.astype(BF16))
    k_gain_row = jnp.concatenate([jnp.tile(k_gain.astype(F32), n_heads), jnp.ones((hd,), F32)]).reshape(1, 2 * hd)
    q_gain_row = jnp.tile(q_gain[0].astype(F32), n_heads).reshape(1, hd)

    def shared_kv(x2d):
        kv = _rms_matmul(x2d, norm_kv, w_kv_main, head_norm_cols=hd, gain_row=k_gain_row)
        logf = _forget_gate(x2d, norm_kv, w_f, b_f_pad)
        return kv[:, :hd], kv[:, hd:], logf

    def slots(a):
        a = a.reshape(a.shape[:-1] + (n_heads, dh))
        a = jnp.pad(a, [(0, 0)] * (a.ndim - 1) + [(0, LANES - dh)])
        return a.reshape(a.shape[:-2] + (n_heads * LANES,))

    spare = jnp.arange(3)
    place = jnp.zeros((LANES, n_heads * LANES), F32).at[
        (spare[:, None] * n_heads + jnp.arange(n_heads)[None, :]).reshape(-1),
        (jnp.arange(n_heads)[None, :] * LANES + dh + spare[:, None]).reshape(-1)].set(1.0).astype(BF16)
    ones_row = slots(jnp.zeros((1, hd), F32)).at[:, (jnp.arange(n_heads)[:, None] * LANES + dh
                                                     + spare[None, :]).reshape(-1)].set(1.0)

    def shared_kv_and_attend_prompt(x2d):
        m = x2d.shape[0]
        k2d, v2d, logf = shared_kv(x2d)
        lf_t = jnp.transpose(logf[:, :n_heads].reshape(batch, seq, n_heads), (0, 2, 1))
        c_terms = _lane_cumsum(lf_t.reshape(batch * n_heads, seq), split_negated=True)
        c_terms = [jnp.transpose(c.reshape(batch, n_heads, seq), (0, 2, 1)).reshape(m, n_heads) for c in c_terms]
        parts = jnp.pad(jnp.concatenate(c_terms, axis=1), ((0, 0), (0, LANES - 3 * n_heads)))
        ka = _rms_matmul(x2d, norm_kv, slots(w_kv_main[:, :hd]), out_dtype=BF16, head_norm_cols=n_heads * LANES,
                         head_slot=LANES, gain_row=slots(k_gain_row[:, :hd]),
                         extra=(parts, place, jnp.zeros_like(ones_row)))
        qa = _rms_matmul(x2d, norm_mix[1], slots(w_q), out_dtype=BF16, head_norm_cols=n_heads * LANES,
                         head_slot=LANES, gain_row=slots(q_gain_row), out_scale=FOX_DH ** -0.5,
                         extra=(jnp.zeros_like(parts), place, ones_row))
        vt = jnp.transpose(v2d.reshape(batch, seq, n_heads, dh), (0, 2, 3, 1)).reshape(batch, hd, seq)
        return _fox_prompt(qa, ka, vt, batch, seq), (k2d, v2d, logf)

    conv0_p = jnp.zeros((2, batch, CONV_W - 1, d_ff), F32)
    chunk_p = min(128, seq)
    y_p, ret_p, conv_p, (k_p, v_p, lf_p) = _trunk(
        x_prompt, None, conv0_p, 0.0, chunk_p, chunk_p, seq, weights, shared_kv_and_attend_prompt)

    lp_s = SUBLANES
    x_s = jnp.pad(x_sample, ((0, 0), (0, lp_s - n_new), (0, 0)))
    lf_pool_t = jnp.transpose(cache_logf.astype(F32), (0, 2, 1)).reshape(n_pool * n_heads, page)
    c_within = _lane_cumsum(lf_pool_t).reshape(n_pool, n_heads, page)
    cache_kt = jnp.transpose(cache_k, (0, 2, 3, 1)).reshape(n_pool, hd, page)
    cache_vt = jnp.transpose(cache_v, (0, 2, 3, 1)).reshape(n_pool, hd, page)

    def shared_kv_and_attend_sample(x2d):
        k2d, v2d, logf = shared_kv(x2d)
        q2d = _rms_matmul(x2d, norm_mix[1], w_q, head_norm_cols=hd, gain_row=q_gain_row)
        lf_new = logf[:, :n_heads].reshape(db, lp_s, n_heads)[:, :n_new]
        lf_new_t = jnp.pad(jnp.transpose(lf_new, (0, 2, 1)), ((0, 0), (0, 0), (0, page - n_new)))
        c_new = _lane_cumsum(lf_new_t.reshape(db * n_heads, page)).reshape(db, n_heads, page)
        a = _fox_sample(page_table, q2d.reshape(db, lp_s, hd), cache_kt, cache_vt, c_within,
                        k2d.reshape(db, lp_s, hd), v2d.reshape(db, lp_s, hd), c_new, n_new=n_new)
        return a.reshape(db * lp_s, hd).astype(BF16), (k2d, v2d, logf)

    y_s, ret_s, conv_s, (k_s, v_s, lf_s) = _trunk(
        x_s, state_ret[0], state_conv, float(past), lp_s, n_new, n_new, weights, shared_kv_and_attend_sample)

    def heads(a, b, l):
        return a.reshape(b, l, n_heads, dh)

    return (y_p, y_s[:, :n_new], ret_p[None], ret_s[None], conv_p, conv_s,
            heads(k_p, batch, seq), heads(v_p, batch, seq), lf_p[:, :n_heads].reshape(batch, seq, n_heads),
            heads(k_s, db, lp_s)[:, :n_new], heads(v_s, db, lp_s)[:, :n_new],
            lf_s[:, :n_heads].reshape(db, lp_s, n_heads)[:, :n_new])
```

```python
import functools

import jax
import jax.numpy as jnp
from jax import lax
from jax.experimental import pallas as pl
from jax.experimental.pallas import tpu as pltpu

RET_DK = 256
RET_DV = 512
FOX_DH = 64
CONV_W = 3
EPS = 1e-6
ROPE_BASE = 10000.0
NEG_INF = -1e30

LANES = 128
SUBLANES = 8
VMEM_LIMIT_CAP = 56 * 1024 * 1024
VMEM_HEADROOM = 12 * 1024 * 1024
PAGES_PER_STEP = 8
FOX_HEADS_PER_STEP = 4

F32 = jnp.float32
BF16 = jnp.bfloat16


def _vmem_limit(block_bytes, scratch_bytes=0):
    return int(min(2 * block_bytes + scratch_bytes + VMEM_HEADROOM, VMEM_LIMIT_CAP))


def _nbytes(shape, dtype):
    n = 1
    for s in shape:
        n *= s
    return n * jnp.dtype(dtype).itemsize


def _pick_tile(n, candidates):
    for c in candidates:
        if n % c == 0:
            return c
    return n


def _dot(a, b):
    return jnp.dot(a, b, preferred_element_type=F32)


def _dot_nt(a, b):
    return lax.dot_general(a, b, (((1,), (1,)), ((), ())), preferred_element_type=F32)


def _dot_tn(a, b):
    return lax.dot_general(a, b, (((0,), (0,)), ((), ())), preferred_element_type=F32)


def _split_bf16(x):
    hi = x.astype(BF16)
    lo = (x - hi.astype(F32)).astype(BF16)
    return hi, lo


def _rms_mm_kernel(x_ref, g_ref, w_ref, *rest, n_norm_tiles, out_scale, has_extra, n_out):
    h_ref = rest[-1]
    o_refs = rest[-1 - n_out:-1]
    if has_extra:
        grp_ref, grp_t_ref, gain_ref, parts_ref, place_ref, const_ref = rest[:6]
    elif n_norm_tiles:
        grp_ref, grp_t_ref, gain_ref = rest[:3]
    j = pl.program_id(1)

    def store(y):
        for o_ref in o_refs:
            o_ref[...] = y.astype(o_ref.dtype)

    @pl.when(j == 0)
    def _():
        x = x_ref[...]
        ms = jnp.mean(x * x, axis=-1, keepdims=True)
        h_ref[...] = (x * lax.rsqrt(ms + EPS) * g_ref[...]).astype(BF16)

    acc = _dot(h_ref[...], w_ref[...])
    if not n_norm_tiles:
        store(acc)
        return

    @pl.when(j < n_norm_tiles)
    def _():
        ssq = _dot((acc * acc).astype(BF16), grp_ref[...])
        inv = lax.rsqrt(ssq * (1.0 / FOX_DH) + EPS)
        inv_hi, inv_lo = _split_bf16(inv)
        inv_full = _dot(inv_hi, grp_t_ref[...]) + _dot(inv_lo, grp_t_ref[...])
        y = acc * inv_full * gain_ref[...]
        if out_scale != 1.0:
            y = y * out_scale
        if has_extra:
            y = y + _dot(parts_ref[...], place_ref[...]) + const_ref[...]
        store(y)

    @pl.when(j >= n_norm_tiles)
    def _():
        store(acc)


def _rms_matmul(x, g, w, *, out_dtype=F32, head_norm_cols=0, head_slot=FOX_DH, gain_row=None, out_scale=1.0,
                extra=None):
    m, k = x.shape
    out_dtypes = out_dtype if isinstance(out_dtype, tuple) else (out_dtype,)
    n = w.shape[1]
    tm = _pick_tile(m, (1024, 512, 256, 128, 64, 32, 16, 8))
    tn = _pick_tile(n, (512, 256, 128))
    assert head_norm_cols % tn == 0
    n_norm_tiles = head_norm_cols // tn
    in_specs = [
        pl.BlockSpec((tm, k), lambda i, j: (i, 0)),
        pl.BlockSpec((1, k), lambda i, j: (0, 0)),
        pl.BlockSpec((k, tn), lambda i, j: (0, j)),
    ]
    args = [x, g.reshape(1, k).astype(F32), w]
    if n_norm_tiles:
        col = jnp.arange(tn) // head_slot
        grp = (col[:, None] == jnp.arange(LANES)[None, :]).astype(BF16)
        in_specs += [
            pl.BlockSpec((tn, LANES), lambda i, j: (0, 0)),
            pl.BlockSpec((LANES, tn), lambda i, j: (0, 0)),
            pl.BlockSpec((1, tn), lambda i, j: (0, j)),
        ]
        args += [grp, grp.T, gain_row]
    if extra is not None:
        assert n_norm_tiles == n // tn
        in_specs += [
            pl.BlockSpec((tm, LANES), lambda i, j: (i, 0)),
            pl.BlockSpec((LANES, tn), lambda i, j: (0, j)),
            pl.BlockSpec((1, tn), lambda i, j: (0, j)),
        ]
        args += list(extra)
    blocks = (_nbytes((tm, k), x.dtype) + _nbytes((k, tn), w.dtype) + sum(_nbytes((tm, tn), t) for t in out_dtypes)
              + 3 * _nbytes((tn, LANES), BF16) + _nbytes((tm, LANES), BF16))
    outs = pl.pallas_call(
        functools.partial(_rms_mm_kernel, n_norm_tiles=n_norm_tiles, out_scale=out_scale,
                          has_extra=extra is not None, n_out=len(out_dtypes)),
        grid=(m // tm, n // tn),
        in_specs=in_specs,
        out_specs=[pl.BlockSpec((tm, tn), lambda i, j: (i, j))] * len(out_dtypes),
        out_shape=[jax.ShapeDtypeStruct((m, n), t) for t in out_dtypes],
        scratch_shapes=[pltpu.VMEM((tm, k), BF16)],
        compiler_params=pltpu.CompilerParams(
            dimension_semantics=("parallel", "arbitrary"),
            vmem_limit_bytes=_vmem_limit(blocks, _nbytes((tm, k), BF16))),
        name="rms_matmul",
    )(*args)
    return tuple(outs) if isinstance(out_dtype, tuple) else outs[0]


def _forget_gate_kernel(x_ref, g_ref, w_ref, b_ref, o_ref):
    x = x_ref[...]
    ms = jnp.mean(x * x, axis=-1, keepdims=True)
    h = (x * lax.rsqrt(ms + EPS) * g_ref[...]).astype(BF16)
    z = _dot(h, w_ref[...]) + b_ref[...]
    o_ref[...] = jnp.minimum(z, 0.0) - jnp.log1p(jnp.exp(-jnp.abs(z)))


def _forget_gate(x, g, w_f, b_f):
    m, k = x.shape
    tm = _pick_tile(m, (1024, 512, 256, 128, 64, 32, 16, 8))
    blocks = _nbytes((tm, k), F32) + _nbytes((k, LANES), BF16) + _nbytes((tm, LANES), F32)
    return pl.pallas_call(
        _forget_gate_kernel,
        grid=(m // tm,),
        in_specs=[
            pl.BlockSpec((tm, k), lambda i: (i, 0)),
            pl.BlockSpec((1, k), lambda i: (0, 0)),
            pl.BlockSpec((k, LANES), lambda i: (0, 0)),
            pl.BlockSpec((1, LANES), lambda i: (0, 0)),
        ],
        out_specs=pl.BlockSpec((tm, LANES), lambda i: (i, 0)),
        out_shape=jax.ShapeDtypeStruct((m, LANES), F32),
        compiler_params=pltpu.CompilerParams(
            dimension_semantics=("parallel",), vmem_limit_bytes=_vmem_limit(blocks)),
        name="forget_gate",
    )(x, g.reshape(1, k).astype(F32), w_f, b_f)


def _mm_res_kernel(a_ref, w_ref, r_ref, o_ref):
    o_ref[...] = r_ref[...] + _dot(a_ref[...], w_ref[...])


def _matmul_residual(a, w, res):
    m, k = a.shape
    n = w.shape[1]
    tm = _pick_tile(m, (1024, 512, 256, 128, 64, 32, 16, 8))
    tn = _pick_tile(n, (512, 256, 128))
    blocks = _nbytes((tm, k), a.dtype) + _nbytes((k, tn), w.dtype) + 2 * _nbytes((tm, tn), F32)
    return pl.pallas_call(
        _mm_res_kernel,
        grid=(m // tm, n // tn),
        in_specs=[
            pl.BlockSpec((tm, k), lambda i, j: (i, 0)),
            pl.BlockSpec((k, tn), lambda i, j: (0, j)),
            pl.BlockSpec((tm, tn), lambda i, j: (i, j)),
        ],
        out_specs=pl.BlockSpec((tm, tn), lambda i, j: (i, j)),
        out_shape=jax.ShapeDtypeStruct((m, n), F32),
        compiler_params=pltpu.CompilerParams(
            dimension_semantics=("parallel", "parallel"), vmem_limit_bytes=_vmem_limit(blocks)),
        name="matmul_residual",
    )(a, w, res)


def _retention_kernel(lg_ref, q_ref, k_ref, v_ref, g_ref, cos_ref, sin_ref, *rest, chunk, n_heads, has_state):
    if has_state:
        s0_ref, y_ref, s_ref = rest
    else:
        y_ref, s_ref = rest
    c = pl.program_id(1)
    rows = q_ref.shape[1]
    half = RET_DK // 2

    @pl.when(c == 0)
    def _():
        if has_state:
            s_ref[...] = s0_ref[...]
        else:
            s_ref[...] = jnp.zeros_like(s_ref)

    cos = cos_ref[...]
    sin = sin_ref[...]

    def rotary(x):
        x1, x2 = x[:, :half], x[:, half:]
        return jnp.concatenate([x1 * cos - x2 * sin, x2 * cos + x1 * sin], axis=-1)

    ri = lax.broadcasted_iota(jnp.int32, (rows, 1), 0).astype(F32)
    ci = lax.broadcasted_iota(jnp.int32, (1, rows), 1).astype(F32)
    diff = ri - ci
    for h in range(n_heads):
        lg = lg_ref[h]
        q = rotary(q_ref[0, :, h * RET_DK:(h + 1) * RET_DK]) * (RET_DK ** -0.5)
        k = rotary(k_ref[0, :, h * RET_DK:(h + 1) * RET_DK])
        v = v_ref[0, :, h * RET_DV:(h + 1) * RET_DV].astype(BF16)
        q_dec = jnp.exp(lg * (ri + 1.0))
        k_dec = jnp.exp(lg * (chunk - 1.0 - ri))
        dmask = jnp.where(diff >= 0, jnp.exp(lg * jnp.maximum(diff, 0.0)), 0.0)
        s_dec = jnp.exp(lg * jnp.full((1, RET_DV), float(chunk), F32))

        att = _dot_nt(q.astype(BF16), k.astype(BF16)) * dmask
        state = s_ref[0, h]
        o = _dot(att.astype(BF16), v) + _dot((q * q_dec).astype(BF16), state.astype(BF16))
        s_ref[0, h] = state * s_dec + _dot_tn((k * k_dec).astype(BF16), v)

        o = o * lax.rsqrt(jnp.mean(o * o, axis=-1, keepdims=True) + EPS)
        gate = g_ref[0, :, h * RET_DV:(h + 1) * RET_DV].astype(F32)
        y_ref[0, :, h * RET_DV:(h + 1) * RET_DV] = (o * (gate * (1.0 / (1.0 + jnp.exp(-gate))))).astype(y_ref.dtype)


def _retention(qk, vg, lg, cos, sin, state0, *, rows, chunk):
    b, lp, _ = qk.shape
    n_heads = lg.shape[0]
    n_chunks = lp // rows
    qw, vw = n_heads * RET_DK, n_heads * RET_DV
    has_state = state0 is not None
    vg_dtype = vg.dtype
    in_specs = [
        pl.BlockSpec(memory_space=pltpu.SMEM),
        pl.BlockSpec((1, rows, qw), lambda i, c: (i, c, 0)),
        pl.BlockSpec((1, rows, qw), lambda i, c: (i, c, 1)),
        pl.BlockSpec((1, rows, vw), lambda i, c: (i, c, 0)),
        pl.BlockSpec((1, rows, vw), lambda i, c: (i, c, 1)),
        pl.BlockSpec((rows, RET_DK // 2), lambda i, c: (c, 0)),
        pl.BlockSpec((rows, RET_DK // 2), lambda i, c: (c, 0)),
    ]
    args = [lg, qk, qk, vg, vg, cos, sin]
    state_spec = pl.BlockSpec((1, n_heads, RET_DK, RET_DV), lambda i, c: (i, 0, 0, 0))
    if has_state:
        in_specs.append(state_spec)
        args.append(state0)
    blocks = (2 * _nbytes((rows, qw), F32) + 2 * _nbytes((rows, vw), vg_dtype) + _nbytes((rows, RET_DK), F32)
              + _nbytes((rows, vw), BF16) + 2 * _nbytes((n_heads, RET_DK, RET_DV), F32))
    return pl.pallas_call(
        functools.partial(_retention_kernel, chunk=chunk, n_heads=n_heads, has_state=has_state),
        grid=(b, n_chunks),
        in_specs=in_specs,
        out_specs=[pl.BlockSpec((1, rows, vw), lambda i, c: (i, c, 0)), state_spec],
        out_shape=[
            jax.ShapeDtypeStruct((b, lp, vw), BF16),
            jax.ShapeDtypeStruct((b, n_heads, RET_DK, RET_DV), F32),
        ],
        compiler_params=pltpu.CompilerParams(
            dimension_semantics=("parallel", "arbitrary"),
            vmem_limit_bytes=_vmem_limit(blocks)),
        name="retention",
    )(*args)


def _ffn_out_kernel(u_ref, halo_ref, prev_ref, gv_ref, cw_ref, cb_ref, w_ref, r_ref, o_ref, *, chunk):
    i = pl.program_id(1)
    rows, d_ff = u_ref.shape[1], u_ref.shape[2]
    first_tile = i == 0
    row = lax.broadcasted_iota(jnp.int32, (rows, 1), 0)
    o_ref[0] = r_ref[0]
    for c0 in range(0, d_ff, chunk):
        cols = slice(c0, c0 + chunk)
        u = u_ref[0, :, cols]
        halo = jnp.where(first_tile, prev_ref[0, :, cols], halo_ref[0, :, cols])
        h1 = halo[SUBLANES - 1:SUBLANES]
        h2 = halo[SUBLANES - 2:SUBLANES - 1]
        u1 = jnp.where(row >= 1, pltpu.roll(u, 1, axis=0), h1)
        u2 = jnp.where(row >= 2, pltpu.roll(u, 2, axis=0), jnp.where(row == 1, h1, h2))
        cw = cw_ref[:, cols]
        c = cb_ref[:, cols] + cw[0:1] * u2
        c = c + cw[1:2] * u1
        c = c + cw[2:3] * u
        a = (jax.nn.gelu(c) * gv_ref[0, :, cols]).astype(BF16)
        o_ref[0] += _dot(a, w_ref[cols, :])


def _ffn_out(up, prev8, cw8, cb, w_out, res):
    b, lp, _ = up.shape
    d_ff, d = w_out.shape
    tm = _pick_tile(lp, (256, 128, 64, 32, 16, 8))
    chunk = _pick_tile(d_ff, (256, 128))
    halo_per_tile = tm // SUBLANES
    blocks = (2 * _nbytes((tm, d_ff), F32) + 2 * _nbytes((SUBLANES, d_ff), F32) + _nbytes((d_ff, d), BF16)
              + 2 * _nbytes((tm, d), F32) + 2 * _nbytes((SUBLANES, d_ff), F32))
    return pl.pallas_call(
        functools.partial(_ffn_out_kernel, chunk=chunk),
        grid=(b, lp // tm),
        in_specs=[
            pl.BlockSpec((1, tm, d_ff), lambda n, i: (n, i, 0)),
            pl.BlockSpec((1, SUBLANES, d_ff), lambda n, i: (n, jnp.maximum(i * halo_per_tile - 1, 0), 0)),
            pl.BlockSpec((1, SUBLANES, d_ff), lambda n, i: (n, 0, 0)),
            pl.BlockSpec((1, tm, d_ff), lambda n, i: (n, i, 1)),
            pl.BlockSpec((SUBLANES, d_ff), lambda n, i: (0, 0)),
            pl.BlockSpec((1, d_ff), lambda n, i: (0, 0)),
            pl.BlockSpec((d_ff, d), lambda n, i: (0, 0)),
            pl.BlockSpec((1, tm, d), lambda n, i: (n, i, 0)),
        ],
        out_specs=pl.BlockSpec((1, tm, d), lambda n, i: (n, i, 0)),
        out_shape=jax.ShapeDtypeStruct((b, lp, d), F32),
        compiler_params=pltpu.CompilerParams(
            dimension_semantics=("parallel", "parallel"),
            vmem_limit_bytes=_vmem_limit(blocks)),
        name="ffn_out",
    )(up, up, prev8, up, cw8, cb, w_out, res)


def _lane_cumsum_kernel(x_ref, *o_refs, split_negated):
    c = x_ref[...]
    n = c.shape[-1]
    lane = lax.broadcasted_iota(jnp.int32, c.shape, 1)
    shift = 1
    while shift < n:
        c = c + jnp.where(lane >= shift, pltpu.roll(c, shift, axis=1), 0.0)
        shift *= 2
    if not split_negated:
        o_refs[0][...] = c
        return
    hi_ref, mid_ref, lo_ref = o_refs
    hi = (-c).astype(BF16)
    rem = -c - hi.astype(F32)
    mid, lo = _split_bf16(rem)
    hi_ref[...] = hi
    mid_ref[...] = mid
    lo_ref[...] = lo


def _lane_cumsum(x, split_negated=False):
    r, n = x.shape
    tr = _pick_tile(r, (1024, 512, 256, 128, 64, 32, 16, 8))
    spec = pl.BlockSpec((tr, n), lambda i: (i, 0))
    if split_negated:
        out_specs, out_shape = [spec] * 3, [jax.ShapeDtypeStruct((r, n), BF16)] * 3
    else:
        out_specs, out_shape = spec, jax.ShapeDtypeStruct((r, n), F32)
    return pl.pallas_call(
        functools.partial(_lane_cumsum_kernel, split_negated=split_negated),
        grid=(r // tr,),
        in_specs=[spec],
        out_specs=out_specs,
        out_shape=out_shape,
        compiler_params=pltpu.CompilerParams(
            dimension_semantics=("parallel",), vmem_limit_bytes=_vmem_limit(2 * _nbytes((tr, n), F32))),
        name="lane_cumsum",
    )(x)


def _fox_prompt_kernel(qi_ref, ki_ref, qa_ref, ka_ref, vt_ref, o_ref, m_ref, l_ref, acc_ref):
    step_id = pl.program_id(2)
    qi = qi_ref[step_id]
    ki = ki_ref[step_id]
    tq = qa_ref.shape[0]
    tk = ka_ref.shape[0]

    @pl.when(ki == 0)
    def _():
        m_ref[...] = jnp.full_like(m_ref, NEG_INF)
        l_ref[...] = jnp.zeros_like(l_ref)
        acc_ref[...] = jnp.zeros_like(acc_ref)

    n_heads = m_ref.shape[0]

    def step(diagonal):
        for h in range(n_heads):
            s = _dot_nt(ka_ref[:, h * LANES:(h + 1) * LANES], qa_ref[:, h * LANES:(h + 1) * LANES])
            if diagonal:
                key = lax.broadcasted_iota(jnp.int32, (tk, tq), 0)
                qry = lax.broadcasted_iota(jnp.int32, (tk, tq), 1)
                s = jnp.where(key <= qry, s, NEG_INF)
            m_prev = m_ref[h]
            m_new = jnp.maximum(m_prev, jnp.max(s, axis=0, keepdims=True))
            alpha = jnp.exp(m_prev - m_new)
            p = jnp.exp(s - m_new)
            l_ref[h] = alpha * l_ref[h] + jnp.sum(p, axis=0, keepdims=True)
            vt = vt_ref[0, h * FOX_DH:(h + 1) * FOX_DH, :].astype(BF16)
            acc_ref[h] = alpha * acc_ref[h] + _dot(vt, p.astype(BF16))
            m_ref[h] = m_new

    @pl.when(ki < qi)
    def _():
        step(False)

    @pl.when(ki == qi)
    def _():
        step(True)
        out_t = jnp.concatenate([acc_ref[h] / l_ref[h] for h in range(n_heads)], axis=0)
        o_ref[...] = out_t.T.astype(o_ref.dtype)


def _fox_prompt(qa, ka, vt, batch, seq):
    m = qa.shape[0]
    hd = vt.shape[1]
    hps = FOX_HEADS_PER_STEP
    n_groups = hd // (hps * FOX_DH)
    t = _pick_tile(seq, (512, 256, 128))
    nb = seq // t
    tri = [(qi, ki) for qi in range(nb) for ki in range(qi + 1)]
    qi_tab = jnp.asarray([s[0] for s in tri], jnp.int32)
    ki_tab = jnp.asarray([s[1] for s in tri], jnp.int32)
    grid_spec = pltpu.PrefetchScalarGridSpec(
        num_scalar_prefetch=2,
        grid=(batch, n_groups, len(tri)),
        in_specs=[
            pl.BlockSpec((t, hps * LANES), lambda b, p, s, qt, kt: (b * nb + qt[s], p)),
            pl.BlockSpec((t, hps * LANES), lambda b, p, s, qt, kt: (b * nb + kt[s], p)),
            pl.BlockSpec((1, hps * FOX_DH, t), lambda b, p, s, qt, kt: (b, p, kt[s])),
        ],
        out_specs=pl.BlockSpec((t, hps * FOX_DH), lambda b, p, s, qt, kt: (b * nb + qt[s], p)),
        scratch_shapes=[
            pltpu.VMEM((hps, 1, t), F32),
            pltpu.VMEM((hps, 1, t), F32),
            pltpu.VMEM((hps, FOX_DH, t), F32),
        ],
    )
    blocks = (2 * _nbytes((t, hps * LANES), BF16) + _nbytes((hps * FOX_DH, t), F32)
              + _nbytes((t, hps * FOX_DH), BF16))
    scratch = hps * (_nbytes((FOX_DH, t), F32) + 2 * _nbytes((SUBLANES, t), F32))
    return pl.pallas_call(
        _fox_prompt_kernel,
        grid_spec=grid_spec,
        out_shape=jax.ShapeDtypeStruct((m, hd), BF16),
        compiler_params=pltpu.CompilerParams(
            dimension_semantics=("parallel", "parallel", "arbitrary"),
            vmem_limit_bytes=_vmem_limit(blocks, scratch + 6 * _nbytes((t, t), F32))),
        name="fox_prompt",
    )(qi_tab, ki_tab, qa, ka, vt)


def _fox_sample_kernel(pt_ref, q_ref, *refs, n_new, n_heads, group):
    del pt_ref
    kt_refs, vt_refs, cw_refs = refs[:group], refs[group:2 * group], refs[2 * group:3 * group]
    (kn_ref, vn_ref, cn_ref, o_ref, qbd_ref, m_ref, l_ref, acc_ref, carry_ref, kpad_ref, vpad_ref) = refs[3 * group:]
    p = pl.program_id(1)
    n_steps = pl.num_programs(1)
    hd = n_heads * FOX_DH
    page = kt_refs[0].shape[2]
    lane_head = lax.broadcasted_iota(jnp.int32, (n_heads, hd), 1) // FOX_DH
    row_head = lax.broadcasted_iota(jnp.int32, (n_heads, hd), 0)
    own_lanes = lane_head == row_head

    @pl.when(p == 0)
    def _():
        q = q_ref[0]
        for t in range(n_new):
            qt = jnp.broadcast_to(q[t:t + 1] * (FOX_DH ** -0.5), (n_heads, hd))
            qbd_ref[t * n_heads:(t + 1) * n_heads, :] = jnp.where(own_lanes, qt, 0.0).astype(BF16)
        m_ref[...] = jnp.full_like(m_ref, NEG_INF)
        l_ref[...] = jnp.zeros_like(l_ref)
        acc_ref[...] = jnp.zeros_like(acc_ref)
        carry_ref[...] = jnp.zeros_like(carry_ref)

    def softmax_update(s, weighted_values):
        m_prev = m_ref[...]
        m_new = jnp.maximum(m_prev, jnp.max(s, axis=-1, keepdims=True))
        alpha = jnp.exp(m_prev - m_new)
        pr = jnp.exp(s - m_new)
        l_ref[...] = alpha * l_ref[...] + jnp.sum(pr, axis=-1, keepdims=True)
        acc_ref[...] = alpha * acc_ref[...] + weighted_values(pr.astype(BF16))
        m_ref[...] = m_new

    qbd = qbd_ref[...]
    carry = carry_ref[...]
    scores = []
    for g in range(group):
        cw = cw_refs[g][0]
        s = _dot(qbd, kt_refs[g][0].astype(BF16))
        scores.append(s - jnp.concatenate([carry + cw] * n_new, axis=0))
        carry = carry + jnp.broadcast_to(cw[:, page - 1:page], cw.shape)
    carry_ref[...] = carry

    def cached_values(pr):
        out = _dot_nt(pr[:, 0:page], vt_refs[0][0].astype(BF16))
        for g in range(1, group):
            out = out + _dot_nt(pr[:, g * page:(g + 1) * page], vt_refs[g][0].astype(BF16))
        return out

    softmax_update(jnp.concatenate(scores, axis=1), cached_values)

    @pl.when(p == n_steps - 1)
    def _():
        kpad_ref[...] = jnp.zeros_like(kpad_ref)
        vpad_ref[...] = jnp.zeros_like(vpad_ref)
        kpad_ref[0:SUBLANES, :] = kn_ref[0]
        vpad_ref[0:SUBLANES, :] = vn_ref[0]
        key = lax.broadcasted_iota(jnp.int32, (n_new * n_heads, page), 1)
        tok = lax.broadcasted_iota(jnp.int32, (n_new * n_heads, page), 0) // n_heads
        c_keys = jnp.concatenate([carry_ref[...] + cn_ref[0]] * n_new, axis=0)
        s_new = _dot_nt(qbd_ref[...], kpad_ref[...].astype(BF16)) - c_keys
        softmax_update(jnp.where(key <= tok, s_new, NEG_INF), lambda pr: _dot(pr, vpad_ref[...].astype(BF16)))
        out = acc_ref[...] / l_ref[...]
        rows = []
        for t in range(n_new):
            blk = jnp.where(own_lanes, out[t * n_heads:(t + 1) * n_heads], 0.0)
            rows.append(jnp.sum(blk, axis=0, keepdims=True))
        rows.append(jnp.zeros((SUBLANES - n_new, hd), F32))
        o_ref[0] = jnp.concatenate(rows, axis=0).astype(o_ref.dtype)


def _page_index_map(n_pages, group, g):
    return lambda b, p, pt: (pt[b * n_pages + p * group + g], 0, 0)


def _fox_sample(page_table, q8, cache_kt, cache_vt, c_within, k_new8, v_new8, c_new, *, n_new):
    db, n_pages = page_table.shape
    _, hd, page = cache_kt.shape
    n_heads = hd // FOX_DH
    assert page == LANES and n_new <= SUBLANES
    rows = n_new * n_heads
    group = _pick_tile(n_pages, (PAGES_PER_STEP, 4, 2, 1))
    page_specs = [pl.BlockSpec((1, hd, page), _page_index_map(n_pages, group, g)) for g in range(group)]
    cw_specs = [pl.BlockSpec((1, n_heads, page), _page_index_map(n_pages, group, g)) for g in range(group)]
    grid_spec = pltpu.PrefetchScalarGridSpec(
        num_scalar_prefetch=1,
        grid=(db, n_pages // group),
        in_specs=[pl.BlockSpec((1, SUBLANES, hd), lambda b, p, pt: (b, 0, 0))] + page_specs + page_specs + cw_specs + [
            pl.BlockSpec((1, SUBLANES, hd), lambda b, p, pt: (b, 0, 0)),
            pl.BlockSpec((1, SUBLANES, hd), lambda b, p, pt: (b, 0, 0)),
            pl.BlockSpec((1, n_heads, page), lambda b, p, pt: (b, 0, 0)),
        ],
        out_specs=pl.BlockSpec((1, SUBLANES, hd), lambda b, p, pt: (b, 0, 0)),
        scratch_shapes=[
            pltpu.VMEM((rows, hd), BF16),
            pltpu.VMEM((rows, 1), F32),
            pltpu.VMEM((rows, 1), F32),
            pltpu.VMEM((rows, hd), F32),
            pltpu.VMEM((n_heads, page), F32),
            pltpu.VMEM((page, hd), F32),
            pltpu.VMEM((page, hd), F32),
        ],
    )
    blocks = (2 * group * _nbytes((page, hd), F32) + 4 * _nbytes((SUBLANES, hd), F32)
              + (group + 1) * _nbytes((n_heads, page), F32))
    scratch = (_nbytes((rows, hd), BF16) + _nbytes((rows, hd), F32) + 2 * _nbytes((page, hd), F32)
               + 3 * _nbytes((rows, LANES), F32))
    return pl.pallas_call(
        functools.partial(_fox_sample_kernel, n_new=n_new, n_heads=n_heads, group=group),
        grid_spec=grid_spec,
        out_shape=jax.ShapeDtypeStruct((db, SUBLANES, hd), F32),
        compiler_params=pltpu.CompilerParams(
            dimension_semantics=("parallel", "arbitrary"),
            vmem_limit_bytes=_vmem_limit(blocks, scratch)),
        name="fox_sample",
    )(page_table.reshape(-1), q8, *([cache_kt] * group), *([cache_vt] * group), *([c_within] * group),
      k_new8, v_new8, c_new)


def _rotary_tables(pos0, n):
    half = RET_DK // 2
    inv = ROPE_BASE ** (-jnp.linspace(0.0, 1.0, half, dtype=F32))
    pos = pos0 + jnp.arange(n, dtype=F32)
    ang = pos[:, None] * inv[None, :]
    return jnp.cos(ang), jnp.sin(ang)


def _retention_log_decay(n_heads):
    return jnp.log1p(-jnp.exp2(-5.0 - jnp.arange(n_heads, dtype=F32)))


def _pad_rows(a, rows):
    pad = [(0, 0)] * a.ndim
    pad[1] = (rows - a.shape[1], 0)
    return jnp.pad(a, pad)


def _conv_ffn(x2d, batch, lp, n_valid, g, w_in, cw, cb, w_out, prev):
    d_ff = cw.shape[1]
    up = _rms_matmul(x2d, g, w_in).reshape(batch, lp, 2 * d_ff)
    prev8 = _pad_rows(prev.astype(F32), SUBLANES)
    cw8 = jnp.pad(cw.astype(F32), ((0, SUBLANES - CONV_W), (0, 0)))
    new_state = up[:, n_valid - (CONV_W - 1):n_valid, :d_ff]
    out = _ffn_out(up, prev8, cw8, cb.reshape(1, d_ff).astype(F32), w_out, x2d.reshape(batch, lp, -1))
    return out.reshape(batch * lp, -1), new_state


def _trunk(x, ret0, conv0, pos0, rows, chunk, n_valid, vg_dtype, weights, shared_kv_and_attend):
    (norm_mix, norm_ffn, w_ret_in, w_ret_out, w_fox_out, w_ffn_in, conv_w, conv_b, w_ffn_out) = weights
    batch, lp, d = x.shape
    n_ret_heads = d // RET_DK
    x2d = x.reshape(batch * lp, d)

    qk_cols = 2 * n_ret_heads * RET_DK
    qk = _rms_matmul(x2d, norm_mix[0], w_ret_in[:, :qk_cols]).reshape(batch, lp, -1)
    vg = _rms_matmul(x2d, norm_mix[0], w_ret_in[:, qk_cols:], out_dtype=vg_dtype).reshape(batch, lp, -1)
    cos, sin = _rotary_tables(pos0, lp)
    y, ret_state = _retention(qk, vg, _retention_log_decay(n_ret_heads), cos, sin, ret0, rows=rows, chunk=chunk)
    x2d = _matmul_residual(y.reshape(batch * lp, -1), w_ret_out, x2d)
    x2d, conv_state0 = _conv_ffn(x2d, batch, lp, n_valid, norm_ffn[0], w_ffn_in[0], conv_w[0], conv_b[0],
                                 w_ffn_out[0], conv0[0])

    a2d, (k2d, v2d, logf) = shared_kv_and_attend(x2d)
    x2d = _matmul_residual(a2d, w_fox_out, x2d)
    x2d, conv_state1 = _conv_ffn(x2d, batch, lp, n_valid, norm_ffn[1], w_ffn_in[1], conv_w[1], conv_b[1],
                                 w_ffn_out[1], conv0[1])
    return x2d.reshape(batch, lp, d), ret_state, jnp.stack([conv_state0, conv_state1]), (k2d, v2d, logf)


def kernel(x_prompt, x_sample, cache_k, cache_v, cache_logf, state_ret, state_conv, page_table,
           norm_mix, norm_ffn, w_ret_in, w_ret_out, w_fox_q, q_gain, w_fox_out,
           norm_kv, w_kv, b_f, k_gain, w_ffn_in, conv_w, conv_b, w_ffn_out):
    batch, seq, d = x_prompt.shape
    db, n_new, _ = x_sample.shape
    n_pool, page, n_heads, dh = cache_k.shape
    assert dh == FOX_DH and page == LANES and n_new >= CONV_W - 1
    hd = n_heads * dh
    d_ff = conv_w.shape[-1]
    n_pages = page_table.shape[1]
    past = n_pages * page

    w_f = jnp.pad(w_kv[:, 2 * hd:], ((0, 0), (0, LANES - n_heads))).astype(BF16)
    b_f_pad = jnp.pad(b_f.astype(F32), (0, LANES - n_heads)).reshape(1, LANES)
    w_kv_main = w_kv[:, :2 * hd].astype(BF16)
    w_q = w_fox_q[0].astype(BF16)
    weights = (norm_mix, norm_ffn, w_ret_in[0].astype(BF16), w_ret_out[0].astype(BF16), w_fox_out[0].astype(BF16),
               w_ffn_in.astype(BF16), conv_w, conv_b, w_ffn_out.astype(BF16))
    k_gain_row = jnp.concatenate([jnp.tile(k_gain.astype(F32), n_heads), jnp.ones((hd,), F32)]).reshape(1, 2 * hd)
    q_gain_row = jnp.tile(q_gain[0].astype(F32), n_heads).reshape(1, hd)

    def shared_kv(x2d):
        kv = _rms_matmul(x2d, norm_kv, w_kv_main, head_norm_cols=hd, gain_row=k_gain_row)
        logf = _forget_gate(x2d, norm_kv, w_f, b_f_pad)
        return kv[:, :hd], kv[:, hd:], logf

    def slots(a):
        a = a.reshape(a.shape[:-1] + (n_heads, dh))
        a = jnp.pad(a, [(0, 0)] * (a.ndim - 1) + [(0, LANES - dh)])
        return a.reshape(a.shape[:-2] + (n_heads * LANES,))

    spare = jnp.arange(3)
    place = jnp.zeros((LANES, n_heads * LANES), F32).at[
        (spare[:, None] * n_heads + jnp.arange(n_heads)[None, :]).reshape(-1),
        (jnp.arange(n_heads)[None, :] * LANES + dh + spare[:, None]).reshape(-1)].set(1.0).astype(BF16)
    ones_row = slots(jnp.zeros((1, hd), F32)).at[:, (jnp.arange(n_heads)[:, None] * LANES + dh
                                                     + spare[None, :]).reshape(-1)].set(1.0)

    def shared_kv_and_attend_prompt(x2d):
        m = x2d.shape[0]
        logf = _forget_gate(x2d, norm_kv, w_f, b_f_pad)
        lf_t = jnp.transpose(logf[:, :n_heads].reshape(batch, seq, n_heads), (0, 2, 1))
        c_terms = _lane_cumsum(lf_t.reshape(batch * n_heads, seq), split_negated=True)
        c_terms = [jnp.transpose(c.reshape(batch, n_heads, seq), (0, 2, 1)).reshape(m, n_heads) for c in c_terms]
        parts = jnp.pad(jnp.concatenate(c_terms, axis=1), ((0, 0), (0, LANES - 3 * n_heads)))
        k_slots, ka = _rms_matmul(x2d, norm_kv, slots(w_kv_main[:, :hd]), out_dtype=(F32, BF16),
                                  head_norm_cols=n_heads * LANES, head_slot=LANES,
                                  gain_row=slots(k_gain_row[:, :hd]), extra=(parts, place, jnp.zeros_like(ones_row)))
        qa = _rms_matmul(x2d, norm_mix[1], slots(w_q), out_dtype=BF16, head_norm_cols=n_heads * LANES,
                         head_slot=LANES, gain_row=slots(q_gain_row), out_scale=FOX_DH ** -0.5,
                         extra=(jnp.zeros_like(parts), place, ones_row))
        v2d = _rms_matmul(x2d, norm_kv, w_kv_main[:, hd:])
        v4 = v2d.reshape(batch, seq, n_heads, dh)
        vt = jnp.transpose(v4, (0, 2, 3, 1)).reshape(batch, hd, seq)
        k4 = k_slots.reshape(batch, seq, n_heads, LANES)[..., :dh]
        return _fox_prompt(qa, ka, vt, batch, seq), (k4, v4, logf)

    conv0_p = jnp.zeros((2, batch, CONV_W - 1, d_ff), F32)
    chunk_p = min(128, seq)
    y_p, ret_p, conv_p, (k_p, v_p, lf_p) = _trunk(
        x_prompt, None, conv0_p, 0.0, chunk_p, chunk_p, seq, BF16, weights, shared_kv_and_attend_prompt)

    lp_s = SUBLANES
    x_s = jnp.pad(x_sample, ((0, 0), (0, lp_s - n_new), (0, 0)))
    lf_pool_t = jnp.transpose(cache_logf.astype(F32), (0, 2, 1)).reshape(n_pool * n_heads, page)
    c_within = _lane_cumsum(lf_pool_t).reshape(n_pool, n_heads, page)
    cache_kt = jnp.transpose(cache_k, (0, 2, 3, 1)).reshape(n_pool, hd, page)
    cache_vt = jnp.transpose(cache_v, (0, 2, 3, 1)).reshape(n_pool, hd, page)

    def shared_kv_and_attend_sample(x2d):
        k2d, v2d, logf = shared_kv(x2d)
        q2d = _rms_matmul(x2d, norm_mix[1], w_q, head_norm_cols=hd, gain_row=q_gain_row)
        lf_new = logf[:, :n_heads].reshape(db, lp_s, n_heads)[:, :n_new]
        lf_new_t = jnp.pad(jnp.transpose(lf_new, (0, 2, 1)), ((0, 0), (0, 0), (0, page - n_new)))
        c_new = _lane_cumsum(lf_new_t.reshape(db * n_heads, page)).reshape(db, n_heads, page)
        a = _fox_sample(page_table, q2d.reshape(db, lp_s, hd), cache_kt, cache_vt, c_within,
                        k2d.reshape(db, lp_s, hd), v2d.reshape(db, lp_s, hd), c_new, n_new=n_new)
        return a.reshape(db * lp_s, hd).astype(BF16), (k2d, v2d, logf)

    y_s, ret_s, conv_s, (k_s, v_s, lf_s) = _trunk(
        x_s, state_ret[0], state_conv, float(past), lp_s, n_new, n_new, F32, weights, shared_kv_and_attend_sample)

    def heads(a, b, l):
        return a.reshape(b, l, n_heads, dh)

    return (y_p, y_s[:, :n_new], ret_p[None], ret_s[None], conv_p, conv_s,
            k_p, v_p, lf_p[:, :n_heads].reshape(batch, seq, n_heads),
            heads(k_s, db, lp_s)[:, :n_new], heads(v_s, db, lp_s)[:, :n_new],
            lf_s[:, :n_heads].reshape(db, lp_s, n_heads)[:, :n_new])
```

```python
import functools

import jax
import jax.numpy as jnp
from jax import lax
from jax.experimental import pallas as pl
from jax.experimental.pallas import tpu as pltpu

RET_DK = 256
RET_DV = 512
FOX_DH = 64
CONV_W = 3
EPS = 1e-6
ROPE_BASE = 10000.0
NEG_INF = -1e30

LANES = 128
SUBLANES = 8
VMEM_LIMIT_CAP = 56 * 1024 * 1024
VMEM_HEADROOM = 12 * 1024 * 1024
PAGES_PER_STEP = 8
FOX_HEADS_PER_STEP = 4

F32 = jnp.float32
BF16 = jnp.bfloat16


def _vmem_limit(block_bytes, scratch_bytes=0):
    return int(min(2 * block_bytes + scratch_bytes + VMEM_HEADROOM, VMEM_LIMIT_CAP))


def _nbytes(shape, dtype):
    n = 1
    for s in shape:
        n *= s
    return n * jnp.dtype(dtype).itemsize


def _pick_tile(n, candidates):
    for c in candidates:
        if n % c == 0:
            return c
    return n


def _dot(a, b):
    return jnp.dot(a, b, preferred_element_type=F32)


def _dot_nt(a, b):
    return lax.dot_general(a, b, (((1,), (1,)), ((), ())), preferred_element_type=F32)


def _dot_tn(a, b):
    return lax.dot_general(a, b, (((0,), (0,)), ((), ())), preferred_element_type=F32)


def _split_bf16(x):
    hi = x.astype(BF16)
    lo = (x - hi.astype(F32)).astype(BF16)
    return hi, lo


def _rms_mm_kernel(x_ref, g_ref, w_ref, *rest, n_norm_tiles, out_scale, has_extra, out_kinds):
    h_ref = rest[-1]
    o_refs = rest[-1 - len(out_kinds):-1]
    if has_extra:
        grp_ref, grp_t_ref, gain_ref, parts_ref, place_ref, const_ref = rest[:6]
    elif n_norm_tiles:
        grp_ref, grp_t_ref, gain_ref = rest[:3]
    j = pl.program_id(1)

    def store(y):
        for kind, o_ref in zip(out_kinds, o_refs):
            if kind == "plain":
                o_ref[...] = y.astype(o_ref.dtype)
            elif kind == "tail":
                o_ref[...] = y[y.shape[0] - SUBLANES:, :]
            elif kind == "transposed":
                o_ref[0] = y.T
            else:
                y_t = y.T
                o_ref[0] = jnp.concatenate(
                    [y_t[s * LANES:s * LANES + FOX_DH] for s in range(y.shape[1] // LANES)], axis=0)

    @pl.when(j == 0)
    def _():
        x = x_ref[...]
        ms = jnp.mean(x * x, axis=-1, keepdims=True)
        h_ref[...] = (x * lax.rsqrt(ms + EPS) * g_ref[...]).astype(BF16)

    acc = _dot(h_ref[...], w_ref[...])
    if not n_norm_tiles:
        store(acc)
        return

    @pl.when(j < n_norm_tiles)
    def _():
        ssq = _dot((acc * acc).astype(BF16), grp_ref[...])
        inv = lax.rsqrt(ssq * (1.0 / FOX_DH) + EPS)
        inv_hi, inv_lo = _split_bf16(inv)
        inv_full = _dot(inv_hi, grp_t_ref[...]) + _dot(inv_lo, grp_t_ref[...])
        y = acc * inv_full * gain_ref[...]
        if out_scale != 1.0:
            y = y * out_scale
        if has_extra:
            y = y + _dot(parts_ref[...], place_ref[...]) + const_ref[...]
        store(y)

    @pl.when(j >= n_norm_tiles)
    def _():
        store(acc)


def _rms_matmul(x, g, w, *, out_dtype=F32, head_norm_cols=0, head_slot=FOX_DH, gain_row=None, out_scale=1.0,
                extra=None, seq=None):
    m, k = x.shape
    out_specs_in = out_dtype if isinstance(out_dtype, tuple) else (out_dtype,)
    n = w.shape[1]
    tm = _pick_tile(seq if seq else m, (1024, 512, 256, 128, 64, 32, 16, 8))
    tn = _pick_tile(n, (512, 256, 128))
    tiles_per_seq = (seq // tm) if seq else None
    assert head_norm_cols % tn == 0
    n_norm_tiles = head_norm_cols // tn
    in_specs = [
        pl.BlockSpec((tm, k), lambda i, j: (i, 0)),
        pl.BlockSpec((1, k), lambda i, j: (0, 0)),
        pl.BlockSpec((k, tn), lambda i, j: (0, j)),
    ]
    args = [x, g.reshape(1, k).astype(F32), w]
    if n_norm_tiles:
        col = jnp.arange(tn) // head_slot
        grp = (col[:, None] == jnp.arange(LANES)[None, :]).astype(BF16)
        in_specs += [
            pl.BlockSpec((tn, LANES), lambda i, j: (0, 0)),
            pl.BlockSpec((LANES, tn), lambda i, j: (0, 0)),
            pl.BlockSpec((1, tn), lambda i, j: (0, j)),
        ]
        args += [grp, grp.T, gain_row]
    if extra is not None:
        assert n_norm_tiles == n // tn
        in_specs += [
            pl.BlockSpec((tm, LANES), lambda i, j: (i, 0)),
            pl.BlockSpec((LANES, tn), lambda i, j: (0, j)),
            pl.BlockSpec((1, tn), lambda i, j: (0, j)),
        ]
        args += list(extra)
    out_kinds, out_specs, out_shapes = [], [], []
    for spec in out_specs_in:
        if spec == "tail":
            out_kinds.append("tail")
            out_specs.append(pl.BlockSpec((SUBLANES, tn), lambda i, j: (i, j)))
            out_shapes.append(jax.ShapeDtypeStruct((m // tm * SUBLANES, n), F32))
        elif spec in ("transposed", "transposed_heads"):
            cols = tn if spec == "transposed" else tn // LANES * FOX_DH
            out_kinds.append(spec)
            out_specs.append(pl.BlockSpec((1, cols, tm),
                                          lambda i, j: (i // tiles_per_seq, j, i % tiles_per_seq)))
            out_shapes.append(jax.ShapeDtypeStruct((m // seq, n // tn * cols, seq), F32))
        else:
            out_kinds.append("plain")
            out_specs.append(pl.BlockSpec((tm, tn), lambda i, j: (i, j)))
            out_shapes.append(jax.ShapeDtypeStruct((m, n), spec))
    blocks = (_nbytes((tm, k), x.dtype) + _nbytes((k, tn), w.dtype) + len(out_kinds) * _nbytes((tm, tn), F32)
              + 3 * _nbytes((tn, LANES), BF16) + _nbytes((tm, LANES), BF16))
    outs = pl.pallas_call(
        functools.partial(_rms_mm_kernel, n_norm_tiles=n_norm_tiles, out_scale=out_scale,
                          has_extra=extra is not None, out_kinds=tuple(out_kinds)),
        grid=(m // tm, n // tn),
        in_specs=in_specs,
        out_specs=out_specs,
        out_shape=out_shapes,
        scratch_shapes=[pltpu.VMEM((tm, k), BF16)],
        compiler_params=pltpu.CompilerParams(
            dimension_semantics=("parallel", "arbitrary"),
            vmem_limit_bytes=_vmem_limit(blocks, _nbytes((tm, k), BF16))),
        name="rms_matmul",
    )(*args)
    return tuple(outs) if isinstance(out_dtype, tuple) else outs[0]


def _forget_gate_kernel(x_ref, g_ref, w_ref, b_ref, o_ref):
    x = x_ref[...]
    ms = jnp.mean(x * x, axis=-1, keepdims=True)
    h = (x * lax.rsqrt(ms + EPS) * g_ref[...]).astype(BF16)
    z = _dot(h, w_ref[...]) + b_ref[...]
    o_ref[...] = jnp.minimum(z, 0.0) - jnp.log1p(jnp.exp(-jnp.abs(z)))


def _forget_gate(x, g, w_f, b_f):
    m, k = x.shape
    tm = _pick_tile(m, (1024, 512, 256, 128, 64, 32, 16, 8))
    blocks = _nbytes((tm, k), F32) + _nbytes((k, LANES), BF16) + _nbytes((tm, LANES), F32)
    return pl.pallas_call(
        _forget_gate_kernel,
        grid=(m // tm,),
        in_specs=[
            pl.BlockSpec((tm, k), lambda i: (i, 0)),
            pl.BlockSpec((1, k), lambda i: (0, 0)),
            pl.BlockSpec((k, LANES), lambda i: (0, 0)),
            pl.BlockSpec((1, LANES), lambda i: (0, 0)),
        ],
        out_specs=pl.BlockSpec((tm, LANES), lambda i: (i, 0)),
        out_shape=jax.ShapeDtypeStruct((m, LANES), F32),
        compiler_params=pltpu.CompilerParams(
            dimension_semantics=("parallel",), vmem_limit_bytes=_vmem_limit(blocks)),
        name="forget_gate",
    )(x, g.reshape(1, k).astype(F32), w_f, b_f)


def _mm_res_kernel(a_ref, w_ref, r_ref, o_ref):
    o_ref[...] = r_ref[...] + _dot(a_ref[...], w_ref[...])


def _matmul_residual(a, w, res):
    m, k = a.shape
    n = w.shape[1]
    tm = _pick_tile(m, (1024, 512, 256, 128, 64, 32, 16, 8))
    tn = _pick_tile(n, (512, 256, 128))
    blocks = _nbytes((tm, k), a.dtype) + _nbytes((k, tn), w.dtype) + 2 * _nbytes((tm, tn), F32)
    return pl.pallas_call(
        _mm_res_kernel,
        grid=(m // tm, n // tn),
        in_specs=[
            pl.BlockSpec((tm, k), lambda i, j: (i, 0)),
            pl.BlockSpec((k, tn), lambda i, j: (0, j)),
            pl.BlockSpec((tm, tn), lambda i, j: (i, j)),
        ],
        out_specs=pl.BlockSpec((tm, tn), lambda i, j: (i, j)),
        out_shape=jax.ShapeDtypeStruct((m, n), F32),
        compiler_params=pltpu.CompilerParams(
            dimension_semantics=("parallel", "parallel"), vmem_limit_bytes=_vmem_limit(blocks)),
        name="matmul_residual",
    )(a, w, res)


def _retention_kernel(lg_ref, q_ref, k_ref, v_ref, g_ref, cos_ref, sin_ref, *rest, chunk, n_heads, has_state):
    if has_state:
        s0_ref, y_ref, s_ref = rest
    else:
        y_ref, s_ref = rest
    c = pl.program_id(1)
    rows = q_ref.shape[1]
    half = RET_DK // 2

    @pl.when(c == 0)
    def _():
        if has_state:
            s_ref[...] = s0_ref[...]
        else:
            s_ref[...] = jnp.zeros_like(s_ref)

    cos = cos_ref[...]
    sin = sin_ref[...]

    def rotary(x):
        x1, x2 = x[:, :half], x[:, half:]
        return jnp.concatenate([x1 * cos - x2 * sin, x2 * cos + x1 * sin], axis=-1)

    ri = lax.broadcasted_iota(jnp.int32, (rows, 1), 0).astype(F32)
    ci = lax.broadcasted_iota(jnp.int32, (1, rows), 1).astype(F32)
    diff = ri - ci
    for h in range(n_heads):
        lg = lg_ref[h]
        q = rotary(q_ref[0, :, h * RET_DK:(h + 1) * RET_DK]) * (RET_DK ** -0.5)
        k = rotary(k_ref[0, :, h * RET_DK:(h + 1) * RET_DK])
        v = v_ref[0, :, h * RET_DV:(h + 1) * RET_DV].astype(BF16)
        q_dec = jnp.exp(lg * (ri + 1.0))
        k_dec = jnp.exp(lg * (chunk - 1.0 - ri))
        dmask = jnp.where(diff >= 0, jnp.exp(lg * jnp.maximum(diff, 0.0)), 0.0)
        s_dec = jnp.exp(lg * jnp.full((1, RET_DV), float(chunk), F32))

        att = _dot_nt(q.astype(BF16), k.astype(BF16)) * dmask
        state = s_ref[0, h]
        o = _dot(att.astype(BF16), v) + _dot((q * q_dec).astype(BF16), state.astype(BF16))
        s_ref[0, h] = state * s_dec + _dot_tn((k * k_dec).astype(BF16), v)

        o = o * lax.rsqrt(jnp.mean(o * o, axis=-1, keepdims=True) + EPS)
        gate = g_ref[0, :, h * RET_DV:(h + 1) * RET_DV].astype(F32)
        y_ref[0, :, h * RET_DV:(h + 1) * RET_DV] = (o * (gate * (1.0 / (1.0 + jnp.exp(-gate))))).astype(y_ref.dtype)


def _retention(qk, vg, lg, cos, sin, state0, *, rows, chunk):
    b, lp, _ = qk.shape
    n_heads = lg.shape[0]
    n_chunks = lp // rows
    qw, vw = n_heads * RET_DK, n_heads * RET_DV
    has_state = state0 is not None
    vg_dtype = vg.dtype
    in_specs = [
        pl.BlockSpec(memory_space=pltpu.SMEM),
        pl.BlockSpec((1, rows, qw), lambda i, c: (i, c, 0)),
        pl.BlockSpec((1, rows, qw), lambda i, c: (i, c, 1)),
        pl.BlockSpec((1, rows, vw), lambda i, c: (i, c, 0)),
        pl.BlockSpec((1, rows, vw), lambda i, c: (i, c, 1)),
        pl.BlockSpec((rows, RET_DK // 2), lambda i, c: (c, 0)),
        pl.BlockSpec((rows, RET_DK // 2), lambda i, c: (c, 0)),
    ]
    args = [lg, qk, qk, vg, vg, cos, sin]
    state_spec = pl.BlockSpec((1, n_heads, RET_DK, RET_DV), lambda i, c: (i, 0, 0, 0))
    if has_state:
        in_specs.append(state_spec)
        args.append(state0)
    blocks = (2 * _nbytes((rows, qw), F32) + 2 * _nbytes((rows, vw), vg_dtype) + _nbytes((rows, RET_DK), F32)
              + _nbytes((rows, vw), BF16) + 2 * _nbytes((n_heads, RET_DK, RET_DV), F32))
    return pl.pallas_call(
        functools.partial(_retention_kernel, chunk=chunk, n_heads=n_heads, has_state=has_state),
        grid=(b, n_chunks),
        in_specs=in_specs,
        out_specs=[pl.BlockSpec((1, rows, vw), lambda i, c: (i, c, 0)), state_spec],
        out_shape=[
            jax.ShapeDtypeStruct((b, lp, vw), BF16),
            jax.ShapeDtypeStruct((b, n_heads, RET_DK, RET_DV), F32),
        ],
        compiler_params=pltpu.CompilerParams(
            dimension_semantics=("parallel", "arbitrary"),
            vmem_limit_bytes=_vmem_limit(blocks)),
        name="retention",
    )(*args)


def _ffn_out_kernel(u_ref, halo_ref, prev_ref, gv_ref, cw_ref, cb_ref, w_ref, r_ref, o_ref, *, chunk):
    i = pl.program_id(1)
    rows, d_ff = u_ref.shape[1], u_ref.shape[2]
    first_tile = i == 0
    row = lax.broadcasted_iota(jnp.int32, (rows, 1), 0)
    o_ref[0] = r_ref[0]
    for c0 in range(0, d_ff, chunk):
        cols = slice(c0, c0 + chunk)
        u = u_ref[0, :, cols].astype(F32)
        halo_rows = halo_ref.shape[1]
        halo = jnp.where(first_tile, prev_ref[0, :, cols],
                         halo_ref[0, :, cols].astype(F32)[halo_rows - SUBLANES:])
        h1 = halo[SUBLANES - 1:SUBLANES]
        h2 = halo[SUBLANES - 2:SUBLANES - 1]
        u1 = jnp.where(row >= 1, pltpu.roll(u, 1, axis=0), h1)
        u2 = jnp.where(row >= 2, pltpu.roll(u, 2, axis=0), jnp.where(row == 1, h1, h2))
        cw = cw_ref[:, cols]
        c = cb_ref[:, cols] + cw[0:1] * u2
        c = c + cw[1:2] * u1
        c = c + cw[2:3] * u
        a = (jax.nn.gelu(c) * gv_ref[0, :, cols].astype(F32)).astype(BF16)
        o_ref[0] += _dot(a, w_ref[cols, :])


def _ffn_out(up, prev8, cw8, cb, w_out, res):
    b, lp, _ = up.shape
    d_ff, d = w_out.shape
    tm = _pick_tile(lp, (256, 128, 64, 32, 16, 8))
    chunk = _pick_tile(d_ff, (256, 128))
    halo_rows = SUBLANES * 4 // jnp.dtype(up.dtype).itemsize
    halo_per_tile = tm // halo_rows
    blocks = (2 * _nbytes((tm, d_ff), up.dtype) + 2 * _nbytes((halo_rows, d_ff), F32) + _nbytes((d_ff, d), BF16)
              + 2 * _nbytes((tm, d), F32) + 2 * _nbytes((SUBLANES, d_ff), F32))
    return pl.pallas_call(
        functools.partial(_ffn_out_kernel, chunk=chunk),
        grid=(b, lp // tm),
        in_specs=[
            pl.BlockSpec((1, tm, d_ff), lambda n, i: (n, i, 0)),
            pl.BlockSpec((1, halo_rows, d_ff), lambda n, i: (n, jnp.maximum(i * halo_per_tile - 1, 0), 0)),
            pl.BlockSpec((1, SUBLANES, d_ff), lambda n, i: (n, 0, 0)),
            pl.BlockSpec((1, tm, d_ff), lambda n, i: (n, i, 1)),
            pl.BlockSpec((SUBLANES, d_ff), lambda n, i: (0, 0)),
            pl.BlockSpec((1, d_ff), lambda n, i: (0, 0)),
            pl.BlockSpec((d_ff, d), lambda n, i: (0, 0)),
            pl.BlockSpec((1, tm, d), lambda n, i: (n, i, 0)),
        ],
        out_specs=pl.BlockSpec((1, tm, d), lambda n, i: (n, i, 0)),
        out_shape=jax.ShapeDtypeStruct((b, lp, d), F32),
        compiler_params=pltpu.CompilerParams(
            dimension_semantics=("parallel", "parallel"),
            vmem_limit_bytes=_vmem_limit(blocks)),
        name="ffn_out",
    )(up, up, prev8, up, cw8, cb, w_out, res)


def _cumsum_lanes(c):
    n = c.shape[-1]
    lane = lax.broadcasted_iota(jnp.int32, c.shape, c.ndim - 1)
    shift = 1
    while shift < n:
        c = c + jnp.where(lane >= shift, pltpu.roll(c, shift, axis=c.ndim - 1), 0.0)
        shift *= 2
    return c


def _lane_cumsum_kernel(x_ref, *o_refs, split_negated):
    c = _cumsum_lanes(x_ref[...])
    if not split_negated:
        o_refs[0][...] = c
        return
    hi_ref, mid_ref, lo_ref = o_refs
    hi = (-c).astype(BF16)
    rem = -c - hi.astype(F32)
    mid, lo = _split_bf16(rem)
    hi_ref[...] = hi
    mid_ref[...] = mid
    lo_ref[...] = lo


def _lane_cumsum(x, split_negated=False):
    r, n = x.shape
    tr = _pick_tile(r, (1024, 512, 256, 128, 64, 32, 16, 8))
    spec = pl.BlockSpec((tr, n), lambda i: (i, 0))
    if split_negated:
        out_specs, out_shape = [spec] * 3, [jax.ShapeDtypeStruct((r, n), BF16)] * 3
    else:
        out_specs, out_shape = spec, jax.ShapeDtypeStruct((r, n), F32)
    return pl.pallas_call(
        functools.partial(_lane_cumsum_kernel, split_negated=split_negated),
        grid=(r // tr,),
        in_specs=[spec],
        out_specs=out_specs,
        out_shape=out_shape,
        compiler_params=pltpu.CompilerParams(
            dimension_semantics=("parallel",), vmem_limit_bytes=_vmem_limit(2 * _nbytes((tr, n), F32))),
        name="lane_cumsum",
    )(x)


def _fox_prompt_kernel(qi_ref, ki_ref, qa_ref, ka_ref, vt_ref, o_ref, m_ref, l_ref, acc_ref):
    step_id = pl.program_id(2)
    qi = qi_ref[step_id]
    ki = ki_ref[step_id]
    tq = qa_ref.shape[0]
    tk = ka_ref.shape[0]

    @pl.when(ki == 0)
    def _():
        m_ref[...] = jnp.full_like(m_ref, NEG_INF)
        l_ref[...] = jnp.zeros_like(l_ref)
        acc_ref[...] = jnp.zeros_like(acc_ref)

    n_heads = m_ref.shape[0]

    def step(diagonal):
        for h in range(n_heads):
            s = _dot_nt(ka_ref[:, h * LANES:(h + 1) * LANES], qa_ref[:, h * LANES:(h + 1) * LANES])
            if diagonal:
                key = lax.broadcasted_iota(jnp.int32, (tk, tq), 0)
                qry = lax.broadcasted_iota(jnp.int32, (tk, tq), 1)
                s = jnp.where(key <= qry, s, NEG_INF)
            m_prev = m_ref[h]
            m_new = jnp.maximum(m_prev, jnp.max(s, axis=0, keepdims=True))
            alpha = jnp.exp(m_prev - m_new)
            p = jnp.exp(s - m_new)
            l_ref[h] = alpha * l_ref[h] + jnp.sum(p, axis=0, keepdims=True)
            vt = vt_ref[0, h * FOX_DH:(h + 1) * FOX_DH, :].astype(BF16)
            acc_ref[h] = alpha * acc_ref[h] + _dot(vt, p.astype(BF16))
            m_ref[h] = m_new

    @pl.when(ki < qi)
    def _():
        step(False)

    @pl.when(ki == qi)
    def _():
        step(True)
        out_t = jnp.concatenate([acc_ref[h] / l_ref[h] for h in range(n_heads)], axis=0)
        o_ref[...] = out_t.T.astype(o_ref.dtype)


def _fox_prompt(qa, ka, vt, batch, seq):
    m = qa.shape[0]
    hd = vt.shape[1]
    hps = FOX_HEADS_PER_STEP
    n_groups = hd // (hps * FOX_DH)
    t = _pick_tile(seq, (512, 256, 128))
    nb = seq // t
    tri = [(qi, ki) for qi in range(nb) for ki in range(qi + 1)]
    qi_tab = jnp.asarray([s[0] for s in tri], jnp.int32)
    ki_tab = jnp.asarray([s[1] for s in tri], jnp.int32)
    grid_spec = pltpu.PrefetchScalarGridSpec(
        num_scalar_prefetch=2,
        grid=(batch, n_groups, len(tri)),
        in_specs=[
            pl.BlockSpec((t, hps * LANES), lambda b, p, s, qt, kt: (b * nb + qt[s], p)),
            pl.BlockSpec((t, hps * LANES), lambda b, p, s, qt, kt: (b * nb + kt[s], p)),
            pl.BlockSpec((1, hps * FOX_DH, t), lambda b, p, s, qt, kt: (b, p, kt[s])),
        ],
        out_specs=pl.BlockSpec((t, hps * FOX_DH), lambda b, p, s, qt, kt: (b * nb + qt[s], p)),
        scratch_shapes=[
            pltpu.VMEM((hps, 1, t), F32),
            pltpu.VMEM((hps, 1, t), F32),
            pltpu.VMEM((hps, FOX_DH, t), F32),
        ],
    )
    blocks = (2 * _nbytes((t, hps * LANES), BF16) + _nbytes((hps * FOX_DH, t), F32)
              + _nbytes((t, hps * FOX_DH), BF16))
    scratch = hps * (_nbytes((FOX_DH, t), F32) + 2 * _nbytes((SUBLANES, t), F32))
    return pl.pallas_call(
        _fox_prompt_kernel,
        grid_spec=grid_spec,
        out_shape=jax.ShapeDtypeStruct((m, hd), BF16),
        compiler_params=pltpu.CompilerParams(
            dimension_semantics=("parallel", "parallel", "arbitrary"),
            vmem_limit_bytes=_vmem_limit(blocks, scratch + 6 * _nbytes((t, t), F32))),
        name="fox_prompt",
    )(qi_tab, ki_tab, qa, ka, vt)


def _fox_sample_kernel(pt_ref, q_ref, *refs, n_new, n_heads, group):
    del pt_ref
    kt_refs, vt_refs, cw_refs = refs[:group], refs[group:2 * group], refs[2 * group:3 * group]
    (kn_ref, vn_ref, cn_ref, o_ref, qbd_ref, m_ref, l_ref, acc_ref, carry_ref, kpad_ref, vpad_ref) = refs[3 * group:]
    p = pl.program_id(1)
    n_steps = pl.num_programs(1)
    hd = n_heads * FOX_DH
    page = kt_refs[0].shape[2]
    lane_head = lax.broadcasted_iota(jnp.int32, (n_heads, hd), 1) // FOX_DH
    row_head = lax.broadcasted_iota(jnp.int32, (n_heads, hd), 0)
    own_lanes = lane_head == row_head

    @pl.when(p == 0)
    def _():
        q = q_ref[0]
        for t in range(n_new):
            qt = jnp.broadcast_to(q[t:t + 1] * (FOX_DH ** -0.5), (n_heads, hd))
            qbd_ref[t * n_heads:(t + 1) * n_heads, :] = jnp.where(own_lanes, qt, 0.0).astype(BF16)
        m_ref[...] = jnp.full_like(m_ref, NEG_INF)
        l_ref[...] = jnp.zeros_like(l_ref)
        acc_ref[...] = jnp.zeros_like(acc_ref)
        carry_ref[...] = jnp.zeros_like(carry_ref)

    def softmax_update(s, weighted_values):
        m_prev = m_ref[...]
        m_new = jnp.maximum(m_prev, jnp.max(s, axis=-1, keepdims=True))
        alpha = jnp.exp(m_prev - m_new)
        pr = jnp.exp(s - m_new)
        l_ref[...] = alpha * l_ref[...] + jnp.sum(pr, axis=-1, keepdims=True)
        acc_ref[...] = alpha * acc_ref[...] + weighted_values(pr.astype(BF16))
        m_ref[...] = m_new

    qbd = qbd_ref[...]
    carry = carry_ref[...]
    scores = []
    for g in range(group):
        cw = _cumsum_lanes(cw_refs[g][0])
        s = _dot(qbd, kt_refs[g][0].astype(BF16))
        scores.append(s - jnp.concatenate([carry + cw] * n_new, axis=0))
        carry = carry + jnp.broadcast_to(cw[:, page - 1:page], cw.shape)
    carry_ref[...] = carry

    def cached_values(pr):
        out = _dot_nt(pr[:, 0:page], vt_refs[0][0].astype(BF16))
        for g in range(1, group):
            out = out + _dot_nt(pr[:, g * page:(g + 1) * page], vt_refs[g][0].astype(BF16))
        return out

    softmax_update(jnp.concatenate(scores, axis=1), cached_values)

    @pl.when(p == n_steps - 1)
    def _():
        kpad_ref[...] = jnp.zeros_like(kpad_ref)
        vpad_ref[...] = jnp.zeros_like(vpad_ref)
        kpad_ref[0:SUBLANES, :] = kn_ref[0]
        vpad_ref[0:SUBLANES, :] = vn_ref[0]
        key = lax.broadcasted_iota(jnp.int32, (n_new * n_heads, page), 1)
        tok = lax.broadcasted_iota(jnp.int32, (n_new * n_heads, page), 0) // n_heads
        c_keys = jnp.concatenate([carry_ref[...] + _cumsum_lanes(cn_ref[0])] * n_new, axis=0)
        s_new = _dot_nt(qbd_ref[...], kpad_ref[...].astype(BF16)) - c_keys
        softmax_update(jnp.where(key <= tok, s_new, NEG_INF), lambda pr: _dot(pr, vpad_ref[...].astype(BF16)))
        out = acc_ref[...] / l_ref[...]
        rows = []
        for t in range(n_new):
            blk = jnp.where(own_lanes, out[t * n_heads:(t + 1) * n_heads], 0.0)
            rows.append(jnp.sum(blk, axis=0, keepdims=True))
        rows.append(jnp.zeros((SUBLANES - n_new, hd), F32))
        o_ref[0] = jnp.concatenate(rows, axis=0).astype(o_ref.dtype)


def _page_index_map(n_pages, group, g):
    return lambda b, p, pt: (pt[b * n_pages + p * group + g], 0, 0)


def _fox_sample(page_table, q8, cache_kt, cache_vt, c_within, k_new8, v_new8, c_new, *, n_new):
    db, n_pages = page_table.shape
    _, hd, page = cache_kt.shape
    n_heads = hd // FOX_DH
    assert page == LANES and n_new <= SUBLANES
    rows = n_new * n_heads
    group = _pick_tile(n_pages, (PAGES_PER_STEP, 4, 2, 1))
    page_specs = [pl.BlockSpec((1, hd, page), _page_index_map(n_pages, group, g)) for g in range(group)]
    cw_specs = [pl.BlockSpec((1, n_heads, page), _page_index_map(n_pages, group, g)) for g in range(group)]
    grid_spec = pltpu.PrefetchScalarGridSpec(
        num_scalar_prefetch=1,
        grid=(db, n_pages // group),
        in_specs=[pl.BlockSpec((1, SUBLANES, hd), lambda b, p, pt: (b, 0, 0))] + page_specs + page_specs + cw_specs + [
            pl.BlockSpec((1, SUBLANES, hd), lambda b, p, pt: (b, 0, 0)),
            pl.BlockSpec((1, SUBLANES, hd), lambda b, p, pt: (b, 0, 0)),
            pl.BlockSpec((1, n_heads, page), lambda b, p, pt: (b, 0, 0)),
        ],
        out_specs=pl.BlockSpec((1, SUBLANES, hd), lambda b, p, pt: (b, 0, 0)),
        scratch_shapes=[
            pltpu.VMEM((rows, hd), BF16),
            pltpu.VMEM((rows, 1), F32),
            pltpu.VMEM((rows, 1), F32),
            pltpu.VMEM((rows, hd), F32),
            pltpu.VMEM((n_heads, page), F32),
            pltpu.VMEM((page, hd), F32),
            pltpu.VMEM((page, hd), F32),
        ],
    )
    blocks = (2 * group * _nbytes((page, hd), F32) + 4 * _nbytes((SUBLANES, hd), F32)
              + (group + 1) * _nbytes((n_heads, page), F32))
    scratch = (_nbytes((rows, hd), BF16) + _nbytes((rows, hd), F32) + 2 * _nbytes((page, hd), F32)
               + 3 * _nbytes((rows, LANES), F32))
    return pl.pallas_call(
        functools.partial(_fox_sample_kernel, n_new=n_new, n_heads=n_heads, group=group),
        grid_spec=grid_spec,
        out_shape=jax.ShapeDtypeStruct((db, SUBLANES, hd), F32),
        compiler_params=pltpu.CompilerParams(
            dimension_semantics=("parallel", "arbitrary"),
            vmem_limit_bytes=_vmem_limit(blocks, scratch)),
        name="fox_sample",
    )(page_table.reshape(-1), q8, *([cache_kt] * group), *([cache_vt] * group), *([c_within] * group),
      k_new8, v_new8, c_new)


def _rotary_tables(pos0, n):
    half = RET_DK // 2
    inv = ROPE_BASE ** (-jnp.linspace(0.0, 1.0, half, dtype=F32))
    pos = pos0 + jnp.arange(n, dtype=F32)
    ang = pos[:, None] * inv[None, :]
    return jnp.cos(ang), jnp.sin(ang)


def _retention_log_decay(n_heads):
    return jnp.log1p(-jnp.exp2(-5.0 - jnp.arange(n_heads, dtype=F32)))


def _pad_rows(a, rows):
    pad = [(0, 0)] * a.ndim
    pad[1] = (rows - a.shape[1], 0)
    return jnp.pad(a, pad)


def _conv_ffn(x2d, batch, lp, n_valid, g, w_in, cw, cb, w_out, prev):
    d_ff = cw.shape[1]
    if n_valid == lp and lp % (2 * SUBLANES) == 0:
        up, tails = _rms_matmul(x2d, g, w_in, out_dtype=(BF16, "tail"), seq=lp)
        tiles_per_seq = tails.shape[0] // SUBLANES // batch
        tails = tails.reshape(batch, tiles_per_seq, SUBLANES, 2 * d_ff)
        new_state = tails[:, tiles_per_seq - 1, SUBLANES - (CONV_W - 1):, :d_ff]
        up = up.reshape(batch, lp, 2 * d_ff)
    else:
        up = _rms_matmul(x2d, g, w_in).reshape(batch, lp, 2 * d_ff)
        new_state = up[:, n_valid - (CONV_W - 1):n_valid, :d_ff]
    prev8 = _pad_rows(prev.astype(F32), SUBLANES)
    cw8 = jnp.pad(cw.astype(F32), ((0, SUBLANES - CONV_W), (0, 0)))
    out = _ffn_out(up, prev8, cw8, cb.reshape(1, d_ff).astype(F32), w_out, x2d.reshape(batch, lp, -1))
    return out.reshape(batch * lp, -1), new_state


def _trunk(x, ret0, conv0, pos0, rows, chunk, n_valid, vg_dtype, weights, shared_kv_and_attend):
    (norm_mix, norm_ffn, w_ret_in, w_ret_out, w_fox_out, w_ffn_in, conv_w, conv_b, w_ffn_out) = weights
    batch, lp, d = x.shape
    n_ret_heads = d // RET_DK
    x2d = x.reshape(batch * lp, d)

    qk_cols = 2 * n_ret_heads * RET_DK
    qk = _rms_matmul(x2d, norm_mix[0], w_ret_in[:, :qk_cols]).reshape(batch, lp, -1)
    vg = _rms_matmul(x2d, norm_mix[0], w_ret_in[:, qk_cols:], out_dtype=vg_dtype).reshape(batch, lp, -1)
    cos, sin = _rotary_tables(pos0, lp)
    y, ret_state = _retention(qk, vg, _retention_log_decay(n_ret_heads), cos, sin, ret0, rows=rows, chunk=chunk)
    x2d = _matmul_residual(y.reshape(batch * lp, -1), w_ret_out, x2d)
    x2d, conv_state0 = _conv_ffn(x2d, batch, lp, n_valid, norm_ffn[0], w_ffn_in[0], conv_w[0], conv_b[0],
                                 w_ffn_out[0], conv0[0])

    a2d, (k2d, v2d, logf) = shared_kv_and_attend(x2d)
    x2d = _matmul_residual(a2d, w_fox_out, x2d)
    x2d, conv_state1 = _conv_ffn(x2d, batch, lp, n_valid, norm_ffn[1], w_ffn_in[1], conv_w[1], conv_b[1],
                                 w_ffn_out[1], conv0[1])
    return x2d.reshape(batch, lp, d), ret_state, jnp.stack([conv_state0, conv_state1]), (k2d, v2d, logf)


def kernel(x_prompt, x_sample, cache_k, cache_v, cache_logf, state_ret, state_conv, page_table,
           norm_mix, norm_ffn, w_ret_in, w_ret_out, w_fox_q, q_gain, w_fox_out,
           norm_kv, w_kv, b_f, k_gain, w_ffn_in, conv_w, conv_b, w_ffn_out):
    batch, seq, d = x_prompt.shape
    db, n_new, _ = x_sample.shape
    n_pool, page, n_heads, dh = cache_k.shape
    assert dh == FOX_DH and page == LANES and n_new >= CONV_W - 1
    hd = n_heads * dh
    d_ff = conv_w.shape[-1]
    n_pages = page_table.shape[1]
    past = n_pages * page

    w_f = jnp.pad(w_kv[:, 2 * hd:], ((0, 0), (0, LANES - n_heads))).astype(BF16)
    b_f_pad = jnp.pad(b_f.astype(F32), (0, LANES - n_heads)).reshape(1, LANES)
    w_kv_main = w_kv[:, :2 * hd].astype(BF16)
    w_q = w_fox_q[0].astype(BF16)
    weights = (norm_mix, norm_ffn, w_ret_in[0].astype(BF16), w_ret_out[0].astype(BF16), w_fox_out[0].astype(BF16),
               w_ffn_in.astype(BF16), conv_w, conv_b, w_ffn_out.astype(BF16))
    k_gain_row = jnp.concatenate([jnp.tile(k_gain.astype(F32), n_heads), jnp.ones((hd,), F32)]).reshape(1, 2 * hd)
    q_gain_row = jnp.tile(q_gain[0].astype(F32), n_heads).reshape(1, hd)

    def shared_kv(x2d):
        kv = _rms_matmul(x2d, norm_kv, w_kv_main, head_norm_cols=hd, gain_row=k_gain_row)
        logf = _forget_gate(x2d, norm_kv, w_f, b_f_pad)
        return kv[:, :hd], kv[:, hd:], logf

    def slots(a):
        a = a.reshape(a.shape[:-1] + (n_heads, dh))
        a = jnp.pad(a, [(0, 0)] * (a.ndim - 1) + [(0, LANES - dh)])
        return a.reshape(a.shape[:-2] + (n_heads * LANES,))

    spare = jnp.arange(3)
    place = jnp.zeros((LANES, n_heads * LANES), F32).at[
        (spare[:, None] * n_heads + jnp.arange(n_heads)[None, :]).reshape(-1),
        (jnp.arange(n_heads)[None, :] * LANES + dh + spare[:, None]).reshape(-1)].set(1.0).astype(BF16)
    ones_row = slots(jnp.zeros((1, hd), F32)).at[:, (jnp.arange(n_heads)[:, None] * LANES + dh
                                                     + spare[None, :]).reshape(-1)].set(1.0)

    def shared_kv_and_attend_prompt(x2d):
        m = x2d.shape[0]
        logf = _forget_gate(x2d, norm_kv, w_f, b_f_pad)
        lf_t = jnp.transpose(logf[:, :n_heads].reshape(batch, seq, n_heads), (0, 2, 1))
        c_terms = _lane_cumsum(lf_t.reshape(batch * n_heads, seq), split_negated=True)
        c_terms = [jnp.transpose(c.reshape(batch, n_heads, seq), (0, 2, 1)).reshape(m, n_heads) for c in c_terms]
        parts = jnp.pad(jnp.concatenate(c_terms, axis=1), ((0, 0), (0, LANES - 3 * n_heads)))
        ka, kt = _rms_matmul(x2d, norm_kv, slots(w_kv_main[:, :hd]), out_dtype=(BF16, "transposed_heads"),
                             head_norm_cols=n_heads * LANES, head_slot=LANES, seq=seq,
                             gain_row=slots(k_gain_row[:, :hd]), extra=(parts, place, jnp.zeros_like(ones_row)))
        qa = _rms_matmul(x2d, norm_mix[1], slots(w_q), out_dtype=BF16, head_norm_cols=n_heads * LANES,
                         head_slot=LANES, gain_row=slots(q_gain_row), out_scale=FOX_DH ** -0.5,
                         extra=(jnp.zeros_like(parts), place, ones_row))
        vt = _rms_matmul(x2d, norm_kv, w_kv_main[:, hd:], out_dtype="transposed", seq=seq)
        k4 = jnp.transpose(kt.reshape(batch, n_heads, dh, seq), (0, 3, 1, 2))
        v4 = jnp.transpose(vt.reshape(batch, n_heads, dh, seq), (0, 3, 1, 2))
        return _fox_prompt(qa, ka, vt, batch, seq), (k4, v4, logf)

    conv0_p = jnp.zeros((2, batch, CONV_W - 1, d_ff), F32)
    chunk_p = min(128, seq)
    y_p, ret_p, conv_p, (k_p, v_p, lf_p) = _trunk(
        x_prompt, None, conv0_p, 0.0, chunk_p, chunk_p, seq, BF16, weights, shared_kv_and_attend_prompt)

    lp_s = SUBLANES
    x_s = jnp.pad(x_sample, ((0, 0), (0, lp_s - n_new), (0, 0)))
    lf_pool_t = jnp.transpose(cache_logf.astype(F32), (0, 2, 1))
    cache_kt = jnp.transpose(cache_k, (0, 2, 3, 1)).reshape(n_pool, hd, page)
    cache_vt = jnp.transpose(cache_v, (0, 2, 3, 1)).reshape(n_pool, hd, page)

    def shared_kv_and_attend_sample(x2d):
        k2d, v2d, logf = shared_kv(x2d)
        q2d = _rms_matmul(x2d, norm_mix[1], w_q, head_norm_cols=hd, gain_row=q_gain_row)
        lf_new = logf[:, :n_heads].reshape(db, lp_s, n_heads)[:, :n_new]
        lf_new_t = jnp.pad(jnp.transpose(lf_new, (0, 2, 1)), ((0, 0), (0, 0), (0, page - n_new)))
        a = _fox_sample(page_table, q2d.reshape(db, lp_s, hd), cache_kt, cache_vt, lf_pool_t,
                        k2d.reshape(db, lp_s, hd), v2d.reshape(db, lp_s, hd), lf_new_t, n_new=n_new)
        return a.reshape(db * lp_s, hd).astype(BF16), (k2d, v2d, logf)

    y_s, ret_s, conv_s, (k_s, v_s, lf_s) = _trunk(
        x_s, state_ret[0], state_conv, float(past), lp_s, n_new, n_new, F32, weights, shared_kv_and_attend_sample)

    def heads(a, b, l):
        return a.reshape(b, l, n_heads, dh)

    return (y_p, y_s[:, :n_new], ret_p[None], ret_s[None], conv_p, conv_s,
            k_p, v_p, lf_p[:, :n_heads].reshape(batch, seq, n_heads),
            heads(k_s, db, lp_s)[:, :n_new], heads(v_s, db, lp_s)[:, :n_new],
            lf_s[:, :n_heads].reshape(db, lp_s, n_heads)[:, :n_new])
```
